```python
import jax, jax.numpy as jnp
from jax import lax
import numpy as np

D_MODEL = 2048
BATCH = 2
SEQ = 8192
DEPTH = 2
DEC_BATCH = 32
DEC_SEQ = 64
PAST_LEN = 1024

CHUNK = 64
EPS = 1e-6
NEG_BIG = -1e30
LB_FLOOR = 1e-30
D_BRANCH = 512
N_BRANCH = 4
HA_HEADS = 4
HA_DK = 128
HA_DV = D_BRANCH // HA_HEADS
POOL_WINDOWS = (2, 4, 8, 16)
POOL_GROUP = D_BRANCH // len(POOL_WINDOWS)
POOL_HIST = max(POOL_WINDOWS) - 1
CONV_W = 3
SB_HEADS = 4
SB_DH = D_BRANCH // SB_HEADS
Q_BLOCK = 128
N_IN_SPLITS = 11
D_IN = N_IN_SPLITS * D_BRANCH
D_FF = 5632
N_EXPERTS = 8
TOP_K = 2
N_DENSE = (DEPTH + 1) // 2
N_MOE = DEPTH // 2

kernel_name = "hybrid_hgrn2_pool_conv_stickbreak_stream"


def rmsnorm(x, g):
    x32 = x.astype(jnp.float32)
    y = x32 * lax.rsqrt(jnp.mean(x32 * x32, axis=-1, keepdims=True) + EPS)
    return (y * g.astype(jnp.float32)).astype(x.dtype)


def layer_lower_bounds(p):
    s = jax.nn.softmax(p.astype(jnp.float32), axis=0)
    return jnp.cumsum(s, axis=0) - s[0:1]


def hgrn2_inputs(q_raw, f_raw, i_raw, lb):
    B, T, _ = q_raw.shape
    z = f_raw.astype(jnp.float32)
    log_lb = jnp.log(jnp.maximum(lb, LB_FLOOR))
    logf = jnp.logaddexp(log_lb, jnp.log1p(-lb) + jax.nn.log_sigmoid(z))
    k = (1.0 - lb) * jax.nn.sigmoid(-z)
    q = jax.nn.silu(q_raw.astype(jnp.float32))
    v = i_raw.astype(jnp.float32)
    shp = (B, T, HA_HEADS, HA_DK)
    return q.reshape(shp), k.reshape(shp), logf.reshape(shp), v.reshape(B, T, HA_HEADS, HA_DV)


def hgrn2_chunk(S, q, k, logf, v):
    L = q.shape[1]
    b = jnp.cumsum(logf, axis=1)
    causal = jnp.tril(jnp.ones((L, L), dtype=bool))
    diff = b[:, :, None] - b[:, None, :]
    decay = jnp.exp(jnp.where(causal[None, :, :, None, None], diff, NEG_BIG))
    attn = jnp.einsum('bthc,bshc,btshc->bhts', q, k, decay)
    o = (jnp.einsum('bhts,bshv->bthv', attn, v)
         + jnp.einsum('bthc,bhcv->bthv', q * jnp.exp(b), S))
    b_last = b[:, -1]
    S_new = (jnp.exp(b_last)[..., None] * S
             + jnp.einsum('bshc,bshv->bhcv', k * jnp.exp(b_last[:, None] - b), v))
    return S_new, o


def hgrn2_prompt(q, k, logf, v):
    B, T, H, _ = q.shape
    nc = T // CHUNK

    def to_chunks(a):
        return a.reshape(B, nc, CHUNK, H, a.shape[-1]).swapaxes(0, 1)

    S0 = jnp.zeros((B, H, HA_DK, HA_DV), jnp.float32)
    S_fin, o = lax.scan(lambda S, inp: hgrn2_chunk(S, *inp), S0,
                        (to_chunks(q), to_chunks(k), to_chunks(logf), to_chunks(v)))
    return o.swapaxes(0, 1).reshape(B, T, H, HA_DV), S_fin


def pool_mixer(xb, hist, pos0, w_pool, scale):
    B, T, _ = xb.shape
    x32 = xb.astype(jnp.float32)
    full = jnp.concatenate([hist.astype(jnp.float32), x32], axis=1)
    cs = jnp.concatenate([jnp.zeros((B, 1, D_BRANCH), jnp.float32),
                          jnp.cumsum(full, axis=1)], axis=1)
    pos = pos0 + jnp.arange(T)
    base = POOL_HIST + 1
    means = []
    for g, w in enumerate(POOL_WINDOWS):
        sl = slice(g * POOL_GROUP, (g + 1) * POOL_GROUP)
        s = cs[:, base:base + T, sl] - cs[:, base - w:base - w + T, sl]
        cnt = jnp.minimum(pos + 1, w).astype(jnp.float32)
        means.append(s / cnt[None, :, None])
    d = (jnp.concatenate(means, axis=-1) - x32).reshape(B, T, len(POOL_WINDOWS), POOL_GROUP)
    y = jnp.einsum('btgc,gcd->btgd', d, w_pool.astype(jnp.float32)).reshape(B, T, D_BRANCH)
    return y * scale.astype(jnp.float32), full[:, -POOL_HIST:]


def short_conv(v, bg, cg, hist, w_conv):
    u = (cg * v).astype(jnp.float32)
    T = u.shape[1]
    full = jnp.concatenate([hist.astype(jnp.float32), u], axis=1)
    w32 = w_conv.astype(jnp.float32)
    y = full[:, 0:T] * w32[0]
    for j in range(1, CONV_W):
        y = y + full[:, j:j + T] * w32[j]
    return bg.astype(jnp.float32) * y, full[:, -(CONV_W - 1):]


def sb_block(q, k, v, q_pos0):
    Lq, Lk = q.shape[1], k.shape[1]
    z = jnp.einsum('bqhd,bkhd->bhqk', q, k) * (SB_DH ** -0.5)
    qpos = q_pos0 + jnp.arange(Lq)
    kpos = jnp.arange(Lk)
    mask = kpos[None, :] < qpos[:, None]
    log_keep = jnp.where(mask, jax.nn.log_sigmoid(-z), 0.0)
    after = lax.cumsum(log_keep, axis=3, reverse=True) - log_keep
    a = jnp.where(mask, jnp.exp(jax.nn.log_sigmoid(z) + after), 0.0)
    return jnp.einsum('bhqk,bkhd->bqhd', a, v)


def sb_prompt(q, k, v):
    B, T, H, Dh = q.shape
    nb = T // Q_BLOCK
    qb = q.reshape(B, nb, Q_BLOCK, H, Dh).swapaxes(0, 1)
    o = lax.map(lambda args: sb_block(args[1], k, v, args[0] * Q_BLOCK), (jnp.arange(nb), qb))
    return o.swapaxes(0, 1).reshape(B, T, H, Dh)


def token_mixers(xn, is_prompt, pos0, hg_state, pool_hist, conv_hist, k_past, v_past,
                 w_in, lb, hnorm, w_pool, pool_scale, w_conv, w_branch, w_gate, b_gate, w_out):
    B, T, _ = xn.shape
    proj = xn @ w_in
    qa, fa, ia, ga, xb, vc, bc, cc, qd, kd, vd = jnp.split(proj, N_IN_SPLITS, axis=-1)
    q, k, logf, v = hgrn2_inputs(qa, fa, ia, lb)
    if is_prompt:
        o_a, S_new = hgrn2_prompt(q, k, logf, v)
    else:
        S_new, o_a = hgrn2_chunk(hg_state.astype(jnp.float32), q, k, logf, v)
    o_a = o_a * lax.rsqrt(jnp.mean(o_a * o_a, axis=-1, keepdims=True) + EPS) * hnorm.astype(jnp.float32)
    o_a = o_a.reshape(B, T, D_BRANCH) * jax.nn.silu(ga.astype(jnp.float32))
    o_b, pool_new = pool_mixer(xb, pool_hist, pos0, w_pool, pool_scale)
    o_c, conv_new = short_conv(vc, bc, cc, conv_hist, w_conv)
    shp = (B, T, SB_HEADS, SB_DH)
    kd = kd.reshape(shp)
    vd = vd.reshape(shp)
    q32 = qd.reshape(shp).astype(jnp.float32)
    if is_prompt:
        o_d = sb_prompt(q32, kd.astype(jnp.float32), vd.astype(jnp.float32))
    else:
        k_all = jnp.concatenate([k_past.astype(jnp.float32), kd.astype(jnp.float32)], axis=1)
        v_all = jnp.concatenate([v_past.astype(jnp.float32), vd.astype(jnp.float32)], axis=1)
        o_d = sb_block(q32, k_all, v_all, k_past.shape[1])
    o_d = o_d.reshape(B, T, D_BRANCH)
    branches = jnp.stack([o_a, o_b, o_c, o_d], axis=2).astype(xn.dtype)
    p = jnp.einsum('btnc,ncd->btnd', branches, w_branch)
    gates = jax.nn.sigmoid((xn @ w_gate + b_gate).reshape(B, T, N_BRANCH, D_MODEL))
    y = jnp.sum(gates * p, axis=2) @ w_out
    return y, S_new, pool_new, conv_new, kd, vd


def swiglu(x, w1, w3, w2):
    return (jax.nn.silu(x @ w1) * (x @ w3)) @ w2


def moe_ffn(x, router, we1, we3, we2):
    logits = (x @ router).astype(jnp.float32)
    top_v, top_i = lax.top_k(logits, TOP_K)
    wts = jax.nn.softmax(top_v, axis=-1)
    gate = jnp.sum(jax.nn.one_hot(top_i, N_EXPERTS, dtype=jnp.float32) * wts[..., None], axis=-2)
    y = jnp.zeros_like(x)
    for e in range(N_EXPERTS):
        y = y + gate[..., e:e + 1].astype(x.dtype) * swiglu(x, we1[e], we3[e], we2[e])
    return y


def run_trunk(x, is_prompt, st_hgrn, st_pool, st_conv, c_k, c_v,
              norm_mix, norm_ffn, norm_final, w_in, hgrn_lower_bound, hgrn_out_norm, pool_w,
              pool_scale, conv_w, w_branch, w_gate, b_gate, w_out, ffn_w1, ffn_w3, ffn_w2,
              moe_router, moe_w1, moe_w3, moe_w2):
    B = x.shape[0]
    pos0 = 0 if is_prompt else PAST_LEN
    lbs = layer_lower_bounds(hgrn_lower_bound)
    hs_out, pool_out, conv_out, k_out, v_out = [], [], [], [], []
    for l in range(DEPTH):
        if is_prompt:
            hs, ph, ch, kp, vp = (None, jnp.zeros((B, POOL_HIST, D_BRANCH), x.dtype),
                                  jnp.zeros((B, CONV_W - 1, D_BRANCH), x.dtype), None, None)
        else:
            hs, ph, ch, kp, vp = st_hgrn[l], st_pool[l], st_conv[l], c_k[l], c_v[l]
        y, S_new, pool_new, conv_new, kd, vd = token_mixers(
            rmsnorm(x, norm_mix[l]), is_prompt, pos0, hs, ph, ch, kp, vp,
            w_in[l], lbs[l], hgrn_out_norm[l], pool_w[l], pool_scale[l], conv_w[l],
            w_branch[l], w_gate[l], b_gate[l], w_out[l])
        x = x + y.astype(x.dtype)
        h = rmsnorm(x, norm_ffn[l])
        if l % 2 == 0:
            i = l // 2
            x = x + swiglu(h, ffn_w1[i], ffn_w3[i], ffn_w2[i])
        else:
            i = l // 2
            x = x + moe_ffn(h, moe_router[i], moe_w1[i], moe_w3[i], moe_w2[i])
        hs_out.append(S_new)
        pool_out.append(pool_new)
        conv_out.append(conv_new)
        k_out.append(kd)
        v_out.append(vd)
    y = rmsnorm(x, norm_final)
    return (y, jnp.stack(hs_out), jnp.stack(pool_out), jnp.stack(conv_out),
            jnp.stack(k_out), jnp.stack(v_out))


def setup_inputs(seed: int = 0) -> dict:
    key = jax.random.key(seed)
    ks = jax.random.split(key, 32)
    f32 = jnp.float32

    def nrm(k, shape, scale):
        return jax.random.normal(k, shape, f32) * scale

    D = D_MODEL
    return {
        "x_prompt": nrm(ks[0], (BATCH, SEQ, D), 1.0),
        "x_sample": nrm(ks[1], (DEC_BATCH, DEC_SEQ, D), 1.0),
        "state_hgrn": nrm(ks[2], (DEPTH, DEC_BATCH, HA_HEADS, HA_DK, HA_DV), 0.5),
        "state_pool": nrm(ks[3], (DEPTH, DEC_BATCH, POOL_HIST, D_BRANCH), 1.0),
        "state_conv": nrm(ks[4], (DEPTH, DEC_BATCH, CONV_W - 1, D_BRANCH), 1.0),
        "cache_k": nrm(ks[5], (DEPTH, DEC_BATCH, PAST_LEN, SB_HEADS, SB_DH), 1.0),
        "cache_v": nrm(ks[6], (DEPTH, DEC_BATCH, PAST_LEN, SB_HEADS, SB_DH), 1.0),
        "norm_mix": 1.0 + nrm(ks[7], (DEPTH, D), 0.05),
        "norm_ffn": 1.0 + nrm(ks[8], (DEPTH, D), 0.05),
        "norm_final": 1.0 + nrm(ks[9], (D,), 0.05),
        "w_in": nrm(ks[10], (DEPTH, D, D_IN), D ** -0.5),
        "hgrn_lower_bound": nrm(ks[11], (DEPTH, HA_HEADS * HA_DK), 0.1),
        "hgrn_out_norm": 1.0 + nrm(ks[12], (DEPTH, HA_DV), 0.05),
        "pool_w": nrm(ks[13], (DEPTH, len(POOL_WINDOWS), POOL_GROUP, POOL_GROUP), POOL_GROUP ** -0.5),
        "pool_scale": 1.0 + nrm(ks[14], (DEPTH, D_BRANCH), 0.1),
        "conv_w": nrm(ks[15], (DEPTH, CONV_W, D_BRANCH), CONV_W ** -0.5),
        "w_branch": nrm(ks[16], (DEPTH, N_BRANCH, D_BRANCH, D), D_BRANCH ** -0.5),
        "w_gate": nrm(ks[17], (DEPTH, D, N_BRANCH * D), D ** -0.5),
        "b_gate": nrm(ks[18], (DEPTH, N_BRANCH * D), 0.1),
        "w_out": nrm(ks[19], (DEPTH, D, D), 0.5 * D ** -0.5),
        "ffn_w1": nrm(ks[20], (N_DENSE, D, D_FF), D ** -0.5),
        "ffn_w3": nrm(ks[21], (N_DENSE, D, D_FF), D ** -0.5),
        "ffn_w2": nrm(ks[22], (N_DENSE, D_FF, D), D_FF ** -0.5),
        "moe_router": nrm(ks[23], (N_MOE, D, N_EXPERTS), D ** -0.5),
        "moe_w1": nrm(ks[24], (N_MOE, N_EXPERTS, D, D_FF), D ** -0.5),
        "moe_w3": nrm(ks[25], (N_MOE, N_EXPERTS, D, D_FF), D ** -0.5),
        "moe_w2": nrm(ks[26], (N_MOE, N_EXPERTS, D_FF, D), D_FF ** -0.5),
    }


def reference(x_prompt, x_sample, state_hgrn, state_pool, state_conv, cache_k, cache_v,
              norm_mix, norm_ffn, norm_final, w_in, hgrn_lower_bound, hgrn_out_norm, pool_w,
              pool_scale, conv_w, w_branch, w_gate, b_gate, w_out, ffn_w1, ffn_w3, ffn_w2,
              moe_router, moe_w1, moe_w3, moe_w2):
    y_prompt, p_hgrn, p_pool, p_conv, p_k, p_v = run_trunk(
        x_prompt, True, None, None, None, None, None,
        norm_mix, norm_ffn, norm_final, w_in, hgrn_lower_bound, hgrn_out_norm, pool_w,
        pool_scale, conv_w, w_branch, w_gate, b_gate, w_out, ffn_w1, ffn_w3, ffn_w2,
        moe_router, moe_w1, moe_w3, moe_w2)
    y_sample, s_hgrn, s_pool, s_conv, s_k, s_v = run_trunk(
        x_sample, False, state_hgrn, state_pool, state_conv, cache_k, cache_v,
        norm_mix, norm_ffn, norm_final, w_in, hgrn_lower_bound, hgrn_out_norm, pool_w,
        pool_scale, conv_w, w_branch, w_gate, b_gate, w_out, ffn_w1, ffn_w3, ffn_w2,
        moe_router, moe_w1, moe_w3, moe_w2)
    return (y_prompt, y_sample, p_hgrn, p_pool, p_conv, p_k, p_v,
            s_hgrn, s_pool, s_conv, s_k, s_v)
```

```python
import functools

import jax
import jax.numpy as jnp
from jax import lax
from jax.experimental import pallas as pl
from jax.experimental.pallas import tpu as pltpu

F32 = jnp.float32
BF16 = jnp.bfloat16

EPS = 1e-6
NEG_BIG = -1e30
LB_FLOOR = 1e-30
N_HEADS = 4
D_HEAD = 128
D_BRANCH = N_HEADS * D_HEAD
POOL_WINDOWS = (2, 4, 8, 16)
POOL_HALO = 16
CONV_W = 3
CONV_HALO = 8
N_EXPERTS = 8
HGRN_CHUNK = 16
SB_TILE = 128
SB_DONE = -120.0
ROUTER_LANES = 128

V7X_VMEM_LIMIT_BYTES = 56 * 1024 * 1024


def _cparams(*sem):
    return pltpu.CompilerParams(dimension_semantics=sem,
                                vmem_limit_bytes=V7X_VMEM_LIMIT_BYTES)


def _split_bf16(x):
    hi = x.astype(BF16)
    lo = (x - hi.astype(F32)).astype(BF16)
    return hi, lo


def _dot(a, b):
    return jnp.dot(a, b, preferred_element_type=F32)


def _dot_nt(a, b):
    return lax.dot_general(a, b, (((1,), (1,)), ((), ())), preferred_element_type=F32)


def _dot_tn(a, b):
    return lax.dot_general(a, b, (((0,), (0,)), ((), ())), preferred_element_type=F32)


def _rmsnorm_rows(x, g):
    return x * lax.rsqrt(jnp.mean(x * x, axis=-1, keepdims=True) + EPS) * g


def _rms_kernel(x_ref, g_ref, o_ref):
    o_ref[...] = _rmsnorm_rows(x_ref[...], g_ref[...]).astype(o_ref.dtype)


def rmsnorm(x, g, out_dtype, tm):
    n, d = x.shape
    return pl.pallas_call(
        _rms_kernel,
        grid=(n // tm,),
        in_specs=[pl.BlockSpec((tm, d), lambda i: (i, 0)),
                  pl.BlockSpec((1, d), lambda i: (0, 0))],
        out_specs=pl.BlockSpec((tm, d), lambda i: (i, 0)),
        out_shape=jax.ShapeDtypeStruct((n, d), out_dtype),
        compiler_params=_cparams("parallel"),
        name="rmsnorm",
    )(x, g.reshape(1, d))


def _add_rms_kernel(x_ref, y_ref, g_ref, xo_ref, no_ref):
    x = x_ref[...] + y_ref[...]
    xo_ref[...] = x
    no_ref[...] = _rmsnorm_rows(x, g_ref[...]).astype(no_ref.dtype)


def add_rmsnorm(x, y, g, tm):
    n, d = x.shape
    row = pl.BlockSpec((tm, d), lambda i: (i, 0))
    return pl.pallas_call(
        _add_rms_kernel,
        grid=(n // tm,),
        in_specs=[row, row, pl.BlockSpec((1, d), lambda i: (0, 0))],
        out_specs=[row, row],
        out_shape=[jax.ShapeDtypeStruct((n, d), F32), jax.ShapeDtypeStruct((n, d), BF16)],
        compiler_params=_cparams("parallel"),
        name="add_rmsnorm",
    )(x, y, g.reshape(1, d))


def _rms_router_kernel(x_ref, g_ref, r_ref, h32_ref, route_ref):
    h = _rmsnorm_rows(x_ref[...], g_ref[...])
    h_hi, h_lo = _split_bf16(h)
    h32_ref[...] = h_hi.astype(F32)
    r_hi, r_lo = _split_bf16(r_ref[...])
    logits = _dot(h_hi, r_hi) + (_dot(h_lo, r_hi) + _dot(h_hi, r_lo))
    lane = lax.broadcasted_iota(jnp.int32, logits.shape, 1).astype(F32)
    neg_inf = jnp.float32(-jnp.inf)
    l1 = jnp.where(lane < N_EXPERTS, logits, neg_inf)
    m1 = jnp.max(l1, axis=-1, keepdims=True)
    i1 = jnp.min(jnp.where(l1 == m1, lane, float(ROUTER_LANES)), axis=-1, keepdims=True)
    l2 = jnp.where(lane == i1, neg_inf, l1)
    m2 = jnp.max(l2, axis=-1, keepdims=True)
    i2 = jnp.min(jnp.where(l2 == m2, lane, float(ROUTER_LANES)), axis=-1, keepdims=True)
    e = jnp.exp(m2 - m1)
    w1 = 1.0 / (1.0 + e)
    w2 = e / (1.0 + e)
    route_ref[...] = jnp.where(lane == 0.0, i1, jnp.where(lane == 1.0, i2, jnp.where(
        lane == 2.0, w1, jnp.where(lane == 3.0, w2, 0.0))))


def rmsnorm_router(x, g, router, tm):
    n, d = x.shape
    r_pad = jnp.zeros((d, ROUTER_LANES), F32).at[:, :N_EXPERTS].set(router)
    row = pl.BlockSpec((tm, d), lambda i: (i, 0))
    return pl.pallas_call(
        _rms_router_kernel,
        grid=(n // tm,),
        in_specs=[row, pl.BlockSpec((1, d), lambda i: (0, 0)),
                  pl.BlockSpec((d, ROUTER_LANES), lambda i: (0, 0))],
        out_specs=[row, pl.BlockSpec((tm, ROUTER_LANES), lambda i: (i, 0))],
        out_shape=[jax.ShapeDtypeStruct((n, d), F32), jax.ShapeDtypeStruct((n, ROUTER_LANES), F32)],
        compiler_params=_cparams("parallel"),
        name="rmsnorm_router",
    )(x, g.reshape(1, d), r_pad)


def _proj_kernel(x_ref, w_ref, *o_refs):
    r = _dot(x_ref[...], w_ref[...].astype(BF16))
    for o in o_refs:
        o[...] = r.astype(o.dtype)


def in_proj(xn, w_all, layer, col_start, n_cols, out_dtypes, tm, tn):
    n, k = xn.shape
    cb0 = col_start // tn
    return pl.pallas_call(
        _proj_kernel,
        grid=(n // tm, n_cols // tn),
        in_specs=[pl.BlockSpec((tm, k), lambda i, j: (i, 0)),
                  pl.BlockSpec((None, k, tn), lambda i, j: (layer, 0, cb0 + j))],
        out_specs=[pl.BlockSpec((tm, tn), lambda i, j: (i, j)) for _ in out_dtypes],
        out_shape=[jax.ShapeDtypeStruct((n, n_cols), dt) for dt in out_dtypes],
        compiler_params=_cparams("parallel", "arbitrary"),
        name="in_proj",
    )(xn, w_all)


def _out_proj_kernel(m_ref, w_ref, x_ref, o_ref):
    o_ref[...] = x_ref[...] + _dot(m_ref[...], w_ref[...].astype(BF16))


def out_proj(merged, w_all, layer, x, tm, tn):
    n, k = merged.shape
    d = x.shape[1]
    return pl.pallas_call(
        _out_proj_kernel,
        grid=(n // tm, d // tn),
        in_specs=[pl.BlockSpec((tm, k), lambda i, j: (i, 0)),
                  pl.BlockSpec((None, k, tn), lambda i, j: (layer, 0, j)),
                  pl.BlockSpec((tm, tn), lambda i, j: (i, j))],
        out_specs=pl.BlockSpec((tm, tn), lambda i, j: (i, j)),
        out_shape=jax.ShapeDtypeStruct((n, d), F32),
        compiler_params=_cparams("parallel", "arbitrary"),
        name="out_proj",
    )(merged, w_all, x)


def _gate_merge_kernel(xn_ref, oa_ref, ob_ref, oc_ref, od_ref,
                       wg0_ref, wg1_ref, wg2_ref, wg3_ref,
                       bg0_ref, bg1_ref, bg2_ref, bg3_ref, wb_ref, o_ref):
    xn = xn_ref[...]
    branches = (oa_ref, ob_ref, oc_ref, od_ref)
    gates = (wg0_ref, wg1_ref, wg2_ref, wg3_ref)
    biases = (bg0_ref, bg1_ref, bg2_ref, bg3_ref)
    acc = None
    for n in range(4):
        g = jax.nn.sigmoid(_dot(xn, gates[n][...].astype(BF16)) + biases[n][...])
        p = _dot(branches[n][...], wb_ref[n].astype(BF16))
        acc = g * p if acc is None else acc + g * p
    o_ref[...] = acc.astype(o_ref.dtype)


def gate_merge(xn, branches, w_gate, b_gate, w_branch, layer, tm, tn):
    n, d = xn.shape
    nb = d // tn
    db = branches[0].shape[1]
    b3 = b_gate.reshape(b_gate.shape[0], 1, b_gate.shape[1])
    row = pl.BlockSpec((tm, d), lambda i, j: (i, 0))
    br = pl.BlockSpec((tm, db), lambda i, j: (i, 0))
    wg = [pl.BlockSpec((None, d, tn), functools.partial(lambda i, j, m: (layer, 0, m * nb + j), m=m))
          for m in range(4)]
    bg = [pl.BlockSpec((None, 1, tn), functools.partial(lambda i, j, m: (layer, 0, m * nb + j), m=m))
          for m in range(4)]
    wb = pl.BlockSpec((None, 4, db, tn), lambda i, j: (layer, 0, 0, j))
    return pl.pallas_call(
        _gate_merge_kernel,
        grid=(n // tm, nb),
        in_specs=[row, br, br, br, br] + wg + bg + [wb],
        out_specs=pl.BlockSpec((tm, tn), lambda i, j: (i, j)),
        out_shape=jax.ShapeDtypeStruct((n, d), BF16),
        compiler_params=_cparams("parallel", "arbitrary"),
        name="gate_merge",
    )(xn, *branches, w_gate, w_gate, w_gate, w_gate, b3, b3, b3, b3, w_branch)


def _ffn_step(h, w1_ref, w3_ref, w2_ref, o_ref):
    f = pl.program_id(1)
    a = _dot(h, w1_ref[...].astype(BF16))
    b = _dot(h, w3_ref[...].astype(BF16))
    act = (a * jax.nn.sigmoid(a) * b).astype(BF16)
    y = _dot(act, w2_ref[...].astype(BF16))

    @pl.when(f == 0)
    def _():
        o_ref[...] = y

    @pl.when(f > 0)
    def _():
        o_ref[...] += y


def _ffn_kernel(h_ref, w1_ref, w3_ref, w2_ref, o_ref):
    _ffn_step(h_ref[...], w1_ref, w3_ref, w2_ref, o_ref)


def dense_ffn(h, w1, w3, w2, idx, tm, tf):
    n, d = h.shape
    dff = w1.shape[-1]
    return pl.pallas_call(
        _ffn_kernel,
        grid=(n // tm, dff // tf),
        in_specs=[pl.BlockSpec((tm, d), lambda i, f: (i, 0)),
                  pl.BlockSpec((None, d, tf), lambda i, f: (idx, 0, f)),
                  pl.BlockSpec((None, d, tf), lambda i, f: (idx, 0, f)),
                  pl.BlockSpec((None, tf, d), lambda i, f: (idx, f, 0))],
        out_specs=pl.BlockSpec((tm, d), lambda i, f: (i, 0)),
        out_shape=jax.ShapeDtypeStruct((n, d), F32),
        compiler_params=_cparams("parallel", "arbitrary"),
        name="dense_ffn",
    )(h, w1, w3, w2)


def _grouped_ffn_kernel(te_ref, nu_ref, h_ref, w1_ref, w3_ref, w2_ref, o_ref, hb_ref):
    live = pl.program_id(0) < nu_ref[0]
    first = pl.program_id(1) == 0

    @pl.when(live & first)
    def _():
        hb_ref[...] = h_ref[...].astype(BF16)

    @pl.when(live)
    def _():
        _ffn_step(hb_ref[...], w1_ref, w3_ref, w2_ref, o_ref)

    @pl.when(jnp.logical_not(live) & first)
    def _():
        o_ref[...] = jnp.zeros_like(o_ref)


def grouped_ffn(xg, tile_expert, n_used, w1, w3, w2, idx, tm, tf):
    p, d = xg.shape
    dff = w1.shape[-1]
    nf = dff // tf

    def w_in_map(r, f, te, nu):
        live = r < nu[0]
        return (idx, te[r], 0, jnp.where(live, f, nf - 1))

    def w_out_map(r, f, te, nu):
        live = r < nu[0]
        return (idx, te[r], jnp.where(live, f, nf - 1), 0)

    grid_spec = pltpu.PrefetchScalarGridSpec(
        num_scalar_prefetch=2,
        grid=(p // tm, nf),
        in_specs=[pl.BlockSpec((tm, d), lambda r, f, te, nu: (r, 0)),
                  pl.BlockSpec((None, None, d, tf), w_in_map),
                  pl.BlockSpec((None, None, d, tf), w_in_map),
                  pl.BlockSpec((None, None, tf, d), w_out_map)],
        out_specs=pl.BlockSpec((tm, d), lambda r, f, te, nu: (r, 0)),
        scratch_shapes=[pltpu.VMEM((tm, d), BF16)],
    )
    return pl.pallas_call(
        _grouped_ffn_kernel,
        grid_spec=grid_spec,
        out_shape=jax.ShapeDtypeStruct((p, d), F32),
        compiler_params=_cparams("arbitrary", "arbitrary"),
        name="grouped_ffn",
    )(tile_expert, n_used, xg, w1, w3, w2)


def _dispatch_kernel(d1_ref, d2_ref, h_hbm, z_hbm, xg_hbm, sem, *, tc):
    del z_hbm
    base = pl.program_id(0) * tc

    def copies(t):
        src = h_hbm.at[pl.ds(base + t, 1), :]
        return (pltpu.make_async_copy(src, xg_hbm.at[pl.ds(d1_ref[0, 0, t], 1), :], sem),
                pltpu.make_async_copy(src, xg_hbm.at[pl.ds(d2_ref[0, 0, t], 1), :], sem))

    def start(t, c):
        for cp in copies(t):
            cp.start()
        return c

    def wait(t, c):
        for cp in copies(t):
            cp.wait()
        return c

    lax.fori_loop(0, tc, start, 0)
    lax.fori_loop(0, tc, wait, 0)


def moe_dispatch(h32, d1, d2, p_rows, tc):
    n, d = h32.shape
    nc = n // tc
    idx = pl.BlockSpec((1, 1, tc), lambda c: (c, 0, 0), memory_space=pltpu.SMEM)
    return pl.pallas_call(
        functools.partial(_dispatch_kernel, tc=tc),
        grid=(nc,),
        in_specs=[idx, idx, pl.BlockSpec(memory_space=pl.ANY), pl.BlockSpec(memory_space=pl.ANY)],
        out_specs=pl.BlockSpec(memory_space=pl.ANY),
        out_shape=jax.ShapeDtypeStruct((p_rows, d), F32),
        scratch_shapes=[pltpu.SemaphoreType.DMA],
        input_output_aliases={3: 0},
        compiler_params=_cparams("arbitrary"),
        name="moe_dispatch",
    )(d1.reshape(nc, 1, tc), d2.reshape(nc, 1, tc), h32, jnp.zeros((p_rows, d), F32))


def _combine_kernel(d1_ref, d2_ref, x_ref, r_ref, g_ref, yg_hbm, xo_ref, yo_ref, y1_buf, y2_buf, sem,
                    *, tc):
    def copies(t):
        return (pltpu.make_async_copy(yg_hbm.at[pl.ds(d1_ref[0, 0, t], 1), :],
                                      y1_buf.at[pl.ds(t, 1), :], sem),
                pltpu.make_async_copy(yg_hbm.at[pl.ds(d2_ref[0, 0, t], 1), :],
                                      y2_buf.at[pl.ds(t, 1), :], sem))

    def start(t, c):
        for cp in copies(t):
            cp.start()
        return c

    def wait(t, c):
        for cp in copies(t):
            cp.wait()
        return c

    lax.fori_loop(0, tc, start, 0)
    lax.fori_loop(0, tc, wait, 0)
    route = r_ref[...]
    i1 = route[:, 0:1]
    i2 = route[:, 1:2]
    w1 = route[:, 2:3]
    w2 = route[:, 3:4]
    first_is_low = i1 < i2
    ya = jnp.where(first_is_low, w1 * y1_buf[...], w2 * y2_buf[...])
    yb = jnp.where(first_is_low, w2 * y2_buf[...], w1 * y1_buf[...])
    x = x_ref[...] + (ya + yb)
    xo_ref[...] = x
    yo_ref[...] = _rmsnorm_rows(x, g_ref[...])


def moe_combine(x, route, yg, d1, d2, g, tc):
    n, d = x.shape
    nc = n // tc
    idx = pl.BlockSpec((1, 1, tc), lambda c: (c, 0, 0), memory_space=pltpu.SMEM)
    row = pl.BlockSpec((tc, d), lambda c: (c, 0))
    return pl.pallas_call(
        functools.partial(_combine_kernel, tc=tc),
        grid=(nc,),
        in_specs=[idx, idx, row, pl.BlockSpec((tc, ROUTER_LANES), lambda c: (c, 0)),
                  pl.BlockSpec((1, d), lambda c: (0, 0)), pl.BlockSpec(memory_space=pl.ANY)],
        out_specs=[row, row],
        out_shape=[jax.ShapeDtypeStruct((n, d), F32), jax.ShapeDtypeStruct((n, d), F32)],
        scratch_shapes=[pltpu.VMEM((tc, d), F32), pltpu.VMEM((tc, d), F32),
                        pltpu.SemaphoreType.DMA],
        compiler_params=_cparams("arbitrary"),
        name="moe_combine",
    )(d1.reshape(nc, 1, tc), d2.reshape(nc, 1, tc), x, route, g.reshape(1, d), yg)


def moe_plan(route, tm):
    n = route.shape[0]
    i1 = route[:, 0].astype(jnp.int32)
    i2 = route[:, 1].astype(jnp.int32)
    experts = jnp.arange(N_EXPERTS, dtype=jnp.int32)
    onehot = ((i1[:, None] == experts) | (i2[:, None] == experts)).astype(jnp.int32)
    counts = jnp.sum(onehot, axis=0)
    tiles = (counts + tm - 1) // tm
    tile_end = jnp.cumsum(tiles)
    row_start = (tile_end - tiles) * tm
    pos = jnp.cumsum(onehot, axis=0) - onehot
    dest = row_start[None, :] + pos
    d1 = jnp.take_along_axis(dest, i1[:, None], axis=1)[:, 0]
    d2 = jnp.take_along_axis(dest, i2[:, None], axis=1)[:, 0]
    n_tiles = (2 * n) // tm + N_EXPERTS
    r = jnp.arange(n_tiles, dtype=jnp.int32)
    tile_expert = jnp.minimum(jnp.sum((r[:, None] >= tile_end[None, :]).astype(jnp.int32), axis=1),
                              N_EXPERTS - 1)
    n_used = tile_end[-1:].astype(jnp.int32)
    last_expert = tile_expert[jnp.maximum(n_used[0] - 1, 0)]
    tile_expert = jnp.where(r < n_used[0], tile_expert, last_expert).astype(jnp.int32)
    return d1, d2, tile_expert, n_used, n_tiles * tm


def _hgrn_kernel(p_ref, lb_ref, hn_ref, s0_ref, oa_ref, sout_ref,
                 st_ref, q_s, k_s, v_s, b_s, o_s, *, tt):
    c_len = HGRN_CHUNK
    t = pl.program_id(1)

    @pl.when(t == 0)
    def _():
        for h in range(N_HEADS):
            st_ref[h] = s0_ref[0, h].T

    lb = lb_ref[...]
    keep = 1.0 - lb
    z = p_ref[:, D_BRANCH:2 * D_BRANCH]
    logf = jnp.log(jnp.maximum(lb, LB_FLOOR) + keep * jax.nn.sigmoid(z))
    k_s[...] = keep * jax.nn.sigmoid(-z)
    qa = p_ref[:, 0:D_BRANCH]
    q_s[...] = qa * jax.nn.sigmoid(qa)
    v_s[...] = p_ref[:, 2 * D_BRANCH:3 * D_BRANCH]
    row = lax.broadcasted_iota(jnp.int32, (tt, tt), 0)
    col = lax.broadcasted_iota(jnp.int32, (tt, tt), 1)
    tri = jnp.where(((row ^ col) < c_len) & (col <= row), 1.0, 0.0).astype(BF16)
    lf_hi, lf_lo = _split_bf16(logf)
    b_s[...] = _dot(tri, lf_hi) + _dot(tri, lf_lo)

    ones = jnp.ones((D_HEAD, D_HEAD), BF16)
    row_c = lax.broadcasted_iota(jnp.int32, (c_len, 1), 0)

    def chunk(c, carry):
        r0 = pl.multiple_of(c * c_len, c_len)
        qc = q_s[pl.ds(r0, c_len), :]
        kc = k_s[pl.ds(r0, c_len), :]
        vc = v_s[pl.ds(r0, c_len), :]
        bc = b_s[pl.ds(r0, c_len), :]
        b_last = bc[c_len - 1:c_len, :]
        qt = (qc * jnp.exp(bc)).astype(BF16)
        kt = (kc * jnp.exp(b_last - bc)).astype(BF16)
        decay = jnp.exp(b_last)
        vb = vc.astype(BF16)
        xs = []
        for s in range(c_len):
            diff = jnp.where(row_c >= s, bc - bc[s:s + 1, :], NEG_BIG)
            xs.append(qc * kc[s:s + 1, :] * jnp.exp(diff))
        x_all = jnp.concatenate(xs, axis=0)
        outs = []
        for h in range(N_HEADS):
            hs = slice(h * D_HEAD, (h + 1) * D_HEAD)
            x_hi, x_lo = _split_bf16(x_all[:, hs])
            y = _dot(x_hi, ones) + _dot(x_lo, ones)
            acc = _dot_nt(qt[:, hs], st_ref[h].astype(BF16))
            for s in range(c_len):
                acc = acc + y[s * c_len:(s + 1) * c_len, :] * vc[s:s + 1, hs]
            outs.append(acc)
            st_ref[h] = st_ref[h] * decay[:, hs] + _dot_tn(vb[:, hs], kt[:, hs])
        o_s[pl.ds(r0, c_len), :] = jnp.concatenate(outs, axis=1)
        return carry

    lax.fori_loop(0, tt // c_len, chunk, 0)

    ga = p_ref[:, 3 * D_BRANCH:4 * D_BRANCH]
    gate = ga * jax.nn.sigmoid(ga)
    hn = hn_ref[...]
    for h in range(N_HEADS):
        hs = slice(h * D_HEAD, (h + 1) * D_HEAD)
        oh = o_s[:, hs]
        oh = oh * lax.rsqrt(jnp.mean(oh * oh, axis=-1, keepdims=True) + EPS) * hn
        oa_ref[:, hs] = (oh * gate[:, hs]).astype(oa_ref.dtype)

    @pl.when(t == pl.num_programs(1) - 1)
    def _():
        for h in range(N_HEADS):
            sout_ref[0, h] = st_ref[h].T


def hgrn_mixer(proj, lb, hnorm, s0, row0, n_seq, seq_len, tt):
    nt = seq_len // tt
    rb0 = row0 // tt
    n_rows = n_seq * seq_len
    scr = [pltpu.VMEM((N_HEADS, D_HEAD, D_HEAD), F32)] + [pltpu.VMEM((tt, D_BRANCH), F32)] * 5
    return pl.pallas_call(
        functools.partial(_hgrn_kernel, tt=tt),
        grid=(n_seq, nt),
        in_specs=[pl.BlockSpec((tt, 4 * D_BRANCH), lambda s, t: (rb0 + s * nt + t, 0)),
                  pl.BlockSpec((1, D_BRANCH), lambda s, t: (0, 0)),
                  pl.BlockSpec((1, D_HEAD), lambda s, t: (0, 0)),
                  pl.BlockSpec((1, N_HEADS, D_HEAD, D_HEAD), lambda s, t: (s, 0, 0, 0))],
        out_specs=[pl.BlockSpec((tt, D_BRANCH), lambda s, t: (s * nt + t, 0)),
                   pl.BlockSpec((1, N_HEADS, D_HEAD, D_HEAD), lambda s, t: (s, 0, 0, 0))],
        out_shape=[jax.ShapeDtypeStruct((n_rows, D_BRANCH), BF16),
                   jax.ShapeDtypeStruct((n_seq, N_HEADS, D_HEAD, D_HEAD), F32)],
        scratch_shapes=scr,
        compiler_params=_cparams("parallel", "arbitrary"),
        name="hgrn_mixer",
    )(proj, lb.reshape(1, D_BRANCH), hnorm.reshape(1, D_HEAD), s0)


def _pool_conv_kernel(p_ref, hp_ref, hc_ref, wp_ref, ps_ref, cw_ref,
                      ob_ref, oc_ref, pn_ref, cn_ref, xe, ue, *, tt, pos0):
    t = pl.program_id(1)

    @pl.when(t == 0)
    def _():
        xe[0:POOL_HALO, :] = hp_ref[0]
        ue[0:CONV_HALO, :] = hc_ref[0]

    @pl.when(t > 0)
    def _():
        xe[0:POOL_HALO, :] = xe[tt:tt + POOL_HALO, :]
        ue[0:CONV_HALO, :] = ue[tt:tt + CONV_HALO, :]

    x = p_ref[:, 0:D_BRANCH]
    xe[POOL_HALO:POOL_HALO + tt, :] = x
    u = p_ref[:, 3 * D_BRANCH:4 * D_BRANCH] * p_ref[:, D_BRANCH:2 * D_BRANCH]
    ue[CONV_HALO:CONV_HALO + tt, :] = u

    pos = pos0 + t * tt + lax.broadcasted_iota(jnp.int32, (tt, 1), 0)
    group = D_BRANCH // len(POOL_WINDOWS)
    for g, w in enumerate(POOL_WINDOWS):
        ls = slice(g * group, (g + 1) * group)
        s = x[:, ls]
        for j in range(1, w):
            s = s + xe[POOL_HALO - j:POOL_HALO - j + tt, ls]
        cnt = jnp.minimum(pos + 1, w).astype(F32)
        dlt = s / cnt - x[:, ls]
        y = _dot(dlt.astype(BF16), wp_ref[g].astype(BF16))
        ob_ref[:, ls] = (y * ps_ref[:, ls]).astype(ob_ref.dtype)

    y = ue[CONV_HALO - 2:CONV_HALO - 2 + tt, :] * cw_ref[0:1, :]
    for j in range(1, CONV_W):
        y = y + ue[CONV_HALO - 2 + j:CONV_HALO - 2 + j + tt, :] * cw_ref[j:j + 1, :]
    oc_ref[...] = (p_ref[:, 2 * D_BRANCH:3 * D_BRANCH] * y).astype(oc_ref.dtype)

    pn_ref[0] = xe[tt:tt + POOL_HALO, :]
    cn_ref[0] = ue[tt:tt + CONV_HALO, :]


def pool_conv_mixer(proj, hist_pool, hist_conv, w_pool, pool_scale, conv_w,
                    row0, n_seq, seq_len, tt, pos0):
    nt = seq_len // tt
    rb0 = row0 // tt
    n_rows = n_seq * seq_len
    cw = jnp.zeros((8, D_BRANCH), F32).at[:CONV_W].set(conv_w)
    out_row = pl.BlockSpec((tt, D_BRANCH), lambda s, t: (s * nt + t, 0))
    return pl.pallas_call(
        functools.partial(_pool_conv_kernel, tt=tt, pos0=pos0),
        grid=(n_seq, nt),
        in_specs=[pl.BlockSpec((tt, 4 * D_BRANCH), lambda s, t: (rb0 + s * nt + t, 1)),
                  pl.BlockSpec((1, POOL_HALO, D_BRANCH), lambda s, t: (s, 0, 0)),
                  pl.BlockSpec((1, CONV_HALO, D_BRANCH), lambda s, t: (s, 0, 0)),
                  pl.BlockSpec(w_pool.shape, lambda s, t: (0, 0, 0)),
                  pl.BlockSpec((1, D_BRANCH), lambda s, t: (0, 0)),
                  pl.BlockSpec((8, D_BRANCH), lambda s, t: (0, 0))],
        out_specs=[out_row, out_row,
                   pl.BlockSpec((1, POOL_HALO, D_BRANCH), lambda s, t: (s, 0, 0)),
                   pl.BlockSpec((1, CONV_HALO, D_BRANCH), lambda s, t: (s, 0, 0))],
        out_shape=[jax.ShapeDtypeStruct((n_rows, D_BRANCH), BF16),
                   jax.ShapeDtypeStruct((n_rows, D_BRANCH), BF16),
                   jax.ShapeDtypeStruct((n_seq, POOL_HALO, D_BRANCH), F32),
                   jax.ShapeDtypeStruct((n_seq, CONV_HALO, D_BRANCH), F32)],
        scratch_shapes=[pltpu.VMEM((tt + POOL_HALO, D_BRANCH), F32),
                        pltpu.VMEM((tt + CONV_HALO, D_BRANCH), F32)],
        compiler_params=_cparams("parallel", "arbitrary"),
        name="pool_conv_mixer",
    )(proj, hist_pool, hist_conv, w_pool, pool_scale.reshape(1, D_BRANCH), cw)


def _sb_tile(q_ref, k, v, o_acc, run, masked):
    tq = q_ref.shape[0]
    scale = D_HEAD ** -0.5
    kr = lax.broadcasted_iota(jnp.int32, (SB_TILE, 2 * SB_TILE), 0)
    kc = lax.broadcasted_iota(jnp.int32, (SB_TILE, 2 * SB_TILE), 1)
    sums = jnp.where((kc >= SB_TILE) | (kr > kc), 1.0, 0.0).astype(BF16)
    if masked:
        qi = lax.broadcasted_iota(jnp.int32, (tq, SB_TILE), 0)
        ki = lax.broadcasted_iota(jnp.int32, (tq, SB_TILE), 1)
        mask = ki < qi
    for h in range(N_HEADS):
        hs = slice(h * D_HEAD, (h + 1) * D_HEAD)
        z = _dot_nt(q_ref[:, hs], k[:, hs].astype(BF16)) * scale
        log_keep = -(jnp.maximum(z, 0.0) + jnp.log1p(jnp.exp(-jnp.abs(z))))
        if masked:
            log_keep = jnp.where(mask, log_keep, 0.0)
        lk_hi, lk_lo = _split_bf16(log_keep)
        sm = _dot(lk_hi, sums) + _dot(lk_lo, sums)
        a = jnp.exp(z + log_keep + sm[:, 0:SB_TILE] + run[:, hs])
        if masked:
            a = jnp.where(mask, a, 0.0)
        o_acc[:, hs] += _dot(a.astype(BF16), v[:, hs].astype(BF16))
        run[:, hs] += sm[:, SB_TILE:2 * SB_TILE]


def _sb_finish(o_ref, o_acc):
    o_ref[...] = o_acc[...].astype(o_ref.dtype)


def _sb_alive(run):
    return (jnp.max(run[...]) > SB_DONE).astype(jnp.int32)


def _sb_past_loop(q_ref, kp_ref, vp_ref, o_acc, run, n_tiles):
    def cond(c):
        j, alive = c
        return (j >= 0) & (alive > 0)

    def body(c):
        j, _ = c
        r0 = pl.multiple_of(j * SB_TILE, SB_TILE)
        _sb_tile(q_ref, kp_ref[pl.ds(r0, SB_TILE), :], vp_ref[pl.ds(r0, SB_TILE), :],
                 o_acc, run, masked=False)
        return j - 1, _sb_alive(run)

    lax.while_loop(cond, body, (jnp.int32(n_tiles) - 1, _sb_alive(run)))


def _sb_prompt_kernel(q_ref, k_ref, v_ref, o_ref, o_acc, run):
    i = pl.program_id(1)
    o_acc[...] = jnp.zeros_like(o_acc)
    run[...] = jnp.zeros_like(run)
    r0 = pl.multiple_of(i * SB_TILE, SB_TILE)
    _sb_tile(q_ref, k_ref[pl.ds(r0, SB_TILE), :], v_ref[pl.ds(r0, SB_TILE), :], o_acc, run,
             masked=True)
    _sb_past_loop(q_ref, k_ref, v_ref, o_acc, run, i)
    _sb_finish(o_ref, o_acc)


def sb_prompt_mixer(qkv, n_seq, seq_len):
    nq = seq_len // SB_TILE
    seq = lambda c: pl.BlockSpec((seq_len, D_BRANCH), lambda s, i: (s, c))
    return pl.pallas_call(
        _sb_prompt_kernel,
        grid=(n_seq, nq),
        in_specs=[pl.BlockSpec((SB_TILE, D_BRANCH), lambda s, i: (s * nq + i, 0)), seq(1), seq(2)],
        out_specs=pl.BlockSpec((SB_TILE, D_BRANCH), lambda s, i: (s * nq + i, 0)),
        out_shape=jax.ShapeDtypeStruct((n_seq * seq_len, D_BRANCH), BF16),
        scratch_shapes=[pltpu.VMEM((SB_TILE, D_BRANCH), F32), pltpu.VMEM((SB_TILE, D_BRANCH), F32)],
        compiler_params=_cparams("parallel", "arbitrary"),
        name="sb_prompt_mixer",
    )(qkv, qkv, qkv)


def _sb_sample_kernel(qkv_ref, kp_ref, vp_ref, o_ref, o_acc, run, *, n_past_tiles):
    o_acc[...] = jnp.zeros_like(o_acc)
    run[...] = jnp.zeros_like(run)
    tq = qkv_ref.shape[0]
    q_ref = qkv_ref.at[:, 0:D_BRANCH]
    pad = jnp.zeros((SB_TILE - tq, D_BRANCH), qkv_ref.dtype)
    k_new = jnp.concatenate([qkv_ref[:, D_BRANCH:2 * D_BRANCH], pad], axis=0)
    v_new = jnp.concatenate([qkv_ref[:, 2 * D_BRANCH:3 * D_BRANCH], pad], axis=0)
    _sb_tile(q_ref, k_new, v_new, o_acc, run, masked=True)
    _sb_past_loop(q_ref, kp_ref.at[0], vp_ref.at[0], o_acc, run, n_past_tiles)
    _sb_finish(o_ref, o_acc)


def sb_sample_mixer(qkv, k_past, v_past, row0, n_seq, seq_len):
    past = k_past.shape[1]
    rb0 = row0 // seq_len
    cache = pl.BlockSpec((1, past, D_BRANCH), lambda s: (s, 0, 0))
    return pl.pallas_call(
        functools.partial(_sb_sample_kernel, n_past_tiles=past // SB_TILE),
        grid=(n_seq,),
        in_specs=[pl.BlockSpec((seq_len, 3 * D_BRANCH), lambda s: (rb0 + s, 0)), cache, cache],
        out_specs=pl.BlockSpec((seq_len, D_BRANCH), lambda s: (s, 0)),
        out_shape=jax.ShapeDtypeStruct((n_seq * seq_len, D_BRANCH), BF16),
        scratch_shapes=[pltpu.VMEM((seq_len, D_BRANCH), F32), pltpu.VMEM((seq_len, D_BRANCH), F32)],
        compiler_params=_cparams("parallel"),
        name="sb_sample_mixer",
    )(qkv, k_past, v_past)


def _tile(n, want):
    if n <= want:
        return n
    for t in range(want, 7, -8):
        if n % t == 0:
            return t
    return n


def kernel(x_prompt, x_sample, state_hgrn, state_pool, state_conv, cache_k, cache_v, norm_mix, norm_ffn, norm_final, w_in, hgrn_lower_bound, hgrn_out_norm, pool_w, pool_scale, conv_w, w_branch, w_gate, b_gate, w_out, ffn_w1, ffn_w3, ffn_w2, moe_router, moe_w1, moe_w3, moe_w2):
    depth = w_in.shape[0]
    pb, pt, d = x_prompt.shape
    sb, st, _ = x_sample.shape
    past = cache_k.shape[2]
    n_p = pb * pt
    n_s = sb * st
    n = n_p + n_s
    x = jnp.concatenate([x_prompt.reshape(n_p, d), x_sample.reshape(n_s, d)], axis=0)

    tm_row = _tile(n, 512)
    tm_mm = _tile(n, 2048)
    tm_ffn = _tile(n, 1024)
    tc = _tile(n, 256)
    tt_p = _tile(pt, 256)

    sm = jax.nn.softmax(hgrn_lower_bound.astype(F32), axis=0)
    lbs = jnp.cumsum(sm, axis=0) - sm[0:1]

    zeros_state = jnp.zeros((pb, N_HEADS, D_HEAD, D_HEAD), F32)
    zeros_pool = jnp.zeros((pb, POOL_HALO, D_BRANCH), F32)
    zeros_conv = jnp.zeros((pb, CONV_HALO, D_BRANCH), F32)
    pool_pad = ((0, 0), (0, 0), (POOL_HALO - state_pool.shape[2], 0), (0, 0))
    conv_pad = ((0, 0), (0, 0), (CONV_HALO - state_conv.shape[2], 0), (0, 0))
    sample_pool = jnp.pad(state_pool, pool_pad)
    sample_conv = jnp.pad(state_conv, conv_pad)
    cache_k2 = cache_k.reshape(depth, sb, past, D_BRANCH)
    cache_v2 = cache_v.reshape(depth, sb, past, D_BRANCH)

    outs = {k: [] for k in ("p_h", "p_pool", "p_conv", "p_k", "p_v",
                            "s_h", "s_pool", "s_conv", "s_k", "s_v")}
    xn = rmsnorm(x, norm_mix[0], BF16, tm_row)
    y_final = None
    for l in range(depth):
        (proj,) = in_proj(xn, w_in, l, 0, 8 * D_BRANCH, (F32,), tm_mm, D_BRANCH)
        qkv32, qkv16 = in_proj(xn, w_in, l, 8 * D_BRANCH, 3 * D_BRANCH, (F32, BF16), tm_mm, D_BRANCH)

        oa_p, hs_p = hgrn_mixer(proj, lbs[l], hgrn_out_norm[l], zeros_state, 0, pb, pt, tt_p)
        oa_s, hs_s = hgrn_mixer(proj, lbs[l], hgrn_out_norm[l], state_hgrn[l], n_p, sb, st, st)
        ob_p, oc_p, pool_p, conv_p = pool_conv_mixer(
            proj, zeros_pool, zeros_conv, pool_w[l], pool_scale[l], conv_w[l], 0, pb, pt, tt_p, 0)
        ob_s, oc_s, pool_s, conv_s = pool_conv_mixer(
            proj, sample_pool[l], sample_conv[l], pool_w[l], pool_scale[l], conv_w[l],
            n_p, sb, st, st, past)
        od_p = sb_prompt_mixer(qkv16, pb, pt)
        od_s = sb_sample_mixer(qkv16, cache_k2[l], cache_v2[l], n_p, sb, st)
        branches = [jnp.concatenate([a, b], axis=0)
                    for a, b in ((oa_p, oa_s), (ob_p, ob_s), (oc_p, oc_s), (od_p, od_s))]

        merged = gate_merge(xn, branches, w_gate, b_gate, w_branch, l, tm_ffn, 256)
        x = out_proj(merged, w_out, l, x, tm_mm, D_BRANCH)

        outs["p_h"].append(hs_p)
        outs["s_h"].append(hs_s)
        outs["p_pool"].append(pool_p[:, 1:])
        outs["s_pool"].append(pool_s[:, 1:])
        outs["p_conv"].append(conv_p[:, CONV_HALO - (CONV_W - 1):])
        outs["s_conv"].append(conv_s[:, CONV_HALO - (CONV_W - 1):])
        outs["p_k"].append(qkv32[:n_p, D_BRANCH:2 * D_BRANCH].reshape(pb, pt, N_HEADS, D_HEAD))
        outs["p_v"].append(qkv32[:n_p, 2 * D_BRANCH:].reshape(pb, pt, N_HEADS, D_HEAD))
        outs["s_k"].append(qkv32[n_p:, D_BRANCH:2 * D_BRANCH].reshape(sb, st, N_HEADS, D_HEAD))
        outs["s_v"].append(qkv32[n_p:, 2 * D_BRANCH:].reshape(sb, st, N_HEADS, D_HEAD))

        g_next = norm_mix[l + 1] if l + 1 < depth else norm_final
        if l % 2 == 0:
            h = rmsnorm(x, norm_ffn[l], BF16, tm_row)
            delta = dense_ffn(h, ffn_w1, ffn_w3, ffn_w2, l // 2, tm_ffn, 256)
            if l + 1 < depth:
                x, xn = add_rmsnorm(x, delta, g_next, tm_row)
            else:
                x = x + delta
                y_final = rmsnorm(x, g_next, F32, tm_row)
        else:
            h32, route = rmsnorm_router(x, norm_ffn[l], moe_router[l // 2], tm_row)
            d1, d2, tile_expert, n_used, p_rows = moe_plan(route, tm_ffn)
            xg = moe_dispatch(h32, d1, d2, p_rows, tc)
            yg = grouped_ffn(xg, tile_expert, n_used, moe_w1, moe_w3, moe_w2, l // 2, tm_ffn, 256)
            x, normed = moe_combine(x, route, yg, d1, d2, g_next, tc)
            if l + 1 < depth:
                xn = normed.astype(BF16)
            else:
                y_final = normed

    st_ = lambda k: jnp.stack(outs[k])
    return (y_final[:n_p].reshape(pb, pt, d), y_final[n_p:].reshape(sb, st, d),
            st_("p_h"), st_("p_pool"), st_("p_conv"), st_("p_k"), st_("p_v"),
            st_("s_h"), st_("s_pool"), st_("s_conv"), st_("s_k"), st_("s_v"))
```

```python
import functools

import jax
import jax.numpy as jnp
from jax import lax
from jax.experimental import pallas as pl
from jax.experimental.pallas import tpu as pltpu

F32 = jnp.float32
BF16 = jnp.bfloat16

EPS = 1e-6
NEG_BIG = -1e30
LB_FLOOR = 1e-30
LOG2_E = 1.4426950408889634
N_HEADS = 4
D_HEAD = 128
D_BRANCH = N_HEADS * D_HEAD
POOL_WINDOWS = (2, 4, 8, 16)
POOL_HALO = 16
CONV_W = 3
CONV_HALO = 8
N_EXPERTS = 8
HGRN_CHUNK = 16
HGRN_UNROLL = 4
SB_TILE = 128
SB_DONE = -120.0
ROUTER_LANES = 128

V7X_VMEM_LIMIT_BYTES = 56 * 1024 * 1024

ANY_SPEC = pl.BlockSpec(memory_space=pl.ANY)


def _cparams(*sem):
    return pltpu.CompilerParams(dimension_semantics=sem,
                                vmem_limit_bytes=V7X_VMEM_LIMIT_BYTES)


def _split_bf16(x):
    hi = x.astype(BF16)
    lo = (x - hi.astype(F32)).astype(BF16)
    return hi, lo


def _dot(a, b):
    return jnp.dot(a, b, preferred_element_type=F32)


def _dot_nt(a, b):
    return lax.dot_general(a, b, (((1,), (1,)), ((), ())), preferred_element_type=F32)


def _dot_tn(a, b):
    return lax.dot_general(a, b, (((0,), (0,)), ((), ())), preferred_element_type=F32)


def _silu(a):
    return a * jax.nn.sigmoid(a)


def _rmsnorm_rows(x, g):
    return x * lax.rsqrt(jnp.mean(x * x, axis=-1, keepdims=True) + EPS) * g


def _rms_kernel(x_ref, g_ref, o_ref):
    o_ref[...] = _rmsnorm_rows(x_ref[...], g_ref[...]).astype(o_ref.dtype)


def rmsnorm(x, g, out_dtype, tm):
    n, d = x.shape
    return pl.pallas_call(
        _rms_kernel,
        grid=(n // tm,),
        in_specs=[pl.BlockSpec((tm, d), lambda i: (i, 0)),
                  pl.BlockSpec((1, d), lambda i: (0, 0))],
        out_specs=pl.BlockSpec((tm, d), lambda i: (i, 0)),
        out_shape=jax.ShapeDtypeStruct((n, d), out_dtype),
        compiler_params=_cparams("parallel"),
        name="rmsnorm",
    )(x, g.reshape(1, d))


def _rms_router_kernel(x_ref, g_ref, r_ref, h32_ref, route_ref):
    h = _rmsnorm_rows(x_ref[...], g_ref[...])
    h_hi, h_lo = _split_bf16(h)
    h32_ref[...] = h_hi.astype(F32)
    r_hi, r_lo = _split_bf16(r_ref[...])
    logits = _dot(h_hi, r_hi) + (_dot(h_lo, r_hi) + _dot(h_hi, r_lo))
    lane = lax.broadcasted_iota(jnp.int32, logits.shape, 1).astype(F32)
    neg_inf = jnp.float32(-jnp.inf)
    l1 = jnp.where(lane < N_EXPERTS, logits, neg_inf)
    m1 = jnp.max(l1, axis=-1, keepdims=True)
    i1 = jnp.min(jnp.where(l1 == m1, lane, float(ROUTER_LANES)), axis=-1, keepdims=True)
    l2 = jnp.where(lane == i1, neg_inf, l1)
    m2 = jnp.max(l2, axis=-1, keepdims=True)
    i2 = jnp.min(jnp.where(l2 == m2, lane, float(ROUTER_LANES)), axis=-1, keepdims=True)
    e = jnp.exp(m2 - m1)
    w1 = 1.0 / (1.0 + e)
    w2 = e / (1.0 + e)
    route_ref[...] = jnp.where(lane == 0.0, i1, jnp.where(lane == 1.0, i2, jnp.where(
        lane == 2.0, w1, jnp.where(lane == 3.0, w2, 0.0))))


def rmsnorm_router(x, g, router, tm):
    n, d = x.shape
    r_pad = jnp.zeros((d, ROUTER_LANES), F32).at[:, :N_EXPERTS].set(router)
    row = pl.BlockSpec((tm, d), lambda i: (i, 0))
    return pl.pallas_call(
        _rms_router_kernel,
        grid=(n // tm,),
        in_specs=[row, pl.BlockSpec((1, d), lambda i: (0, 0)),
                  pl.BlockSpec((d, ROUTER_LANES), lambda i: (0, 0))],
        out_specs=[row, pl.BlockSpec((tm, ROUTER_LANES), lambda i: (i, 0))],
        out_shape=[jax.ShapeDtypeStruct((n, d), F32), jax.ShapeDtypeStruct((n, ROUTER_LANES), F32)],
        compiler_params=_cparams("parallel"),
        name="rmsnorm_router",
    )(x, g.reshape(1, d), r_pad)


def _proj_kernel(x_ref, w_ref, o_ref):
    o_ref[...] = _dot(x_ref[...], w_ref[...].astype(BF16)).astype(o_ref.dtype)


def in_proj(xn, w_all, layer, n_cols, tm, tn):
    n, k = xn.shape
    return pl.pallas_call(
        _proj_kernel,
        grid=(n // tm, n_cols // tn),
        in_specs=[pl.BlockSpec((tm, k), lambda i, j: (i, 0)),
                  pl.BlockSpec((None, k, tn), lambda i, j: (layer, 0, j))],
        out_specs=pl.BlockSpec((tm, tn), lambda i, j: (i, j)),
        out_shape=jax.ShapeDtypeStruct((n, n_cols), F32),
        compiler_params=_cparams("parallel", "arbitrary"),
        name="in_proj",
    )(xn, w_all)


def _qkv_kernel(x_ref, w_ref, pk_hbm, pv_hbm, sk_hbm, sv_hbm,
                o16_ref, pk_ref, pv_ref, sk_ref, sv_ref, *, np_tiles):
    del pk_hbm, pv_hbm, sk_hbm, sv_hbm
    r = _dot(x_ref[...], w_ref[...].astype(BF16))
    o16_ref[...] = r.astype(o16_ref.dtype)
    is_prompt = pl.program_id(0) < np_tiles
    j = pl.program_id(1)

    def put(dst):
        for h in range(N_HEADS):
            dst[:, h, :] = r[:, h * D_HEAD:(h + 1) * D_HEAD]

    for col, p_ref, s_ref in ((1, pk_ref, sk_ref), (2, pv_ref, sv_ref)):
        pl.when((j == col) & is_prompt)(functools.partial(put, p_ref))
        pl.when((j == col) & jnp.logical_not(is_prompt))(functools.partial(put, s_ref))


def cache_buffers(depth, n_p, n_s):
    return [jnp.zeros((depth * n, N_HEADS, D_HEAD), F32) for n in (n_p, n_p, n_s, n_s)]


def qkv_proj(xn, w_all, layer, col_start, n_p, caches, tm):
    n, k = xn.shape
    n_s = n - n_p
    np_tiles, ns_tiles = n_p // tm, n_s // tm
    cb0 = col_start // D_BRANCH
    head_blk = (tm, N_HEADS, D_HEAD)
    p_spec = pl.BlockSpec(head_blk, lambda i, j: (layer * np_tiles + jnp.minimum(i, np_tiles - 1), 0, 0))
    s_spec = pl.BlockSpec(head_blk, lambda i, j: (layer * ns_tiles + jnp.maximum(i - np_tiles, 0), 0, 0))
    return pl.pallas_call(
        functools.partial(_qkv_kernel, np_tiles=np_tiles),
        grid=(n // tm, 3),
        in_specs=[pl.BlockSpec((tm, k), lambda i, j: (i, 0)),
                  pl.BlockSpec((None, k, D_BRANCH), lambda i, j: (layer, 0, cb0 + j))]
                 + [ANY_SPEC] * 4,
        out_specs=[pl.BlockSpec((tm, D_BRANCH), lambda i, j: (i, j)), p_spec, p_spec, s_spec, s_spec],
        out_shape=[jax.ShapeDtypeStruct((n, 3 * D_BRANCH), BF16)]
                  + [jax.ShapeDtypeStruct(c.shape, c.dtype) for c in caches],
        input_output_aliases={2: 1, 3: 2, 4: 3, 5: 4},
        compiler_params=_cparams("arbitrary", "arbitrary"),
        name="qkv_proj",
    )(xn, w_all, *caches)


def _out_proj_kernel(m_ref, w_ref, x_ref, o_ref):
    o_ref[...] = x_ref[...] + _dot(m_ref[...], w_ref[...].astype(BF16))


def out_proj(a, w_all, layer, x, tm, tn):
    n, k = a.shape
    d = x.shape[1]
    return pl.pallas_call(
        _out_proj_kernel,
        grid=(n // tm, d // tn),
        in_specs=[pl.BlockSpec((tm, k), lambda i, j: (i, 0)),
                  pl.BlockSpec((None, k, tn), lambda i, j: (layer, 0, j)),
                  pl.BlockSpec((tm, tn), lambda i, j: (i, j))],
        out_specs=pl.BlockSpec((tm, tn), lambda i, j: (i, j)),
        out_shape=jax.ShapeDtypeStruct((n, d), F32),
        compiler_params=_cparams("parallel", "arbitrary"),
        name="out_proj",
    )(a, w_all, x)


def _gate_merge_kernel(xn_ref, oa_ref, ob_ref, oc_ref, od_ref,
                       wg0_ref, wg1_ref, wg2_ref, wg3_ref,
                       bg0_ref, bg1_ref, bg2_ref, bg3_ref, wb_ref, o_ref):
    xn = xn_ref[...]
    branches = (oa_ref, ob_ref, oc_ref, od_ref)
    gates = (wg0_ref, wg1_ref, wg2_ref, wg3_ref)
    biases = (bg0_ref, bg1_ref, bg2_ref, bg3_ref)
    acc = None
    for n in range(4):
        g = jax.nn.sigmoid(_dot(xn, gates[n][...].astype(BF16)) + biases[n][...])
        p = _dot(branches[n][...], wb_ref[n].astype(BF16))
        acc = g * p if acc is None else acc + g * p
    o_ref[...] = acc.astype(o_ref.dtype)


def gate_merge(xn, branches, w_gate, b_gate, w_branch, layer, tm, tn):
    n, d = xn.shape
    nb = d // tn
    db = branches[0].shape[1]
    b3 = b_gate.reshape(b_gate.shape[0], 1, b_gate.shape[1])
    row = pl.BlockSpec((tm, d), lambda i, j: (i, 0))
    br = pl.BlockSpec((tm, db), lambda i, j: (i, 0))
    wg = [pl.BlockSpec((None, d, tn), functools.partial(lambda i, j, m: (layer, 0, m * nb + j), m=m))
          for m in range(4)]
    bg = [pl.BlockSpec((None, 1, tn), functools.partial(lambda i, j, m: (layer, 0, m * nb + j), m=m))
          for m in range(4)]
    wb = pl.BlockSpec((None, 4, db, tn), lambda i, j: (layer, 0, 0, j))
    return pl.pallas_call(
        _gate_merge_kernel,
        grid=(n // tm, nb),
        in_specs=[row, br, br, br, br] + wg + bg + [wb],
        out_specs=pl.BlockSpec((tm, tn), lambda i, j: (i, j)),
        out_shape=jax.ShapeDtypeStruct((n, d), BF16),
        compiler_params=_cparams("parallel", "arbitrary"),
        name="gate_merge",
    )(xn, *branches, w_gate, w_gate, w_gate, w_gate, b3, b3, b3, b3, w_branch)


def _glu_kernel(h_ref, w1_ref, w3_ref, o_ref):
    h = h_ref[...]
    a = _dot(h, w1_ref[...].astype(BF16))
    b = _dot(h, w3_ref[...].astype(BF16))
    o_ref[...] = (_silu(a) * b).astype(o_ref.dtype)


def dense_glu(h, w1, w3, idx, tm, tn):
    n, d = h.shape
    dff = w1.shape[-1]
    return pl.pallas_call(
        _glu_kernel,
        grid=(n // tm, dff // tn),
        in_specs=[pl.BlockSpec((tm, d), lambda i, j: (i, 0)),
                  pl.BlockSpec((None, d, tn), lambda i, j: (idx, 0, j)),
                  pl.BlockSpec((None, d, tn), lambda i, j: (idx, 0, j))],
        out_specs=pl.BlockSpec((tm, tn), lambda i, j: (i, j)),
        out_shape=jax.ShapeDtypeStruct((n, dff), BF16),
        compiler_params=_cparams("parallel", "arbitrary"),
        name="dense_glu",
    )(h, w1, w3)


def _grouped_glu_kernel(te_ref, nu_ref, tok_ref, h_hbm, w1_ref, w3_ref, o_ref,
                        g_buf, hb_ref, sem, *, tm):
    live = pl.program_id(0) < nu_ref[0]

    @pl.when(live & (pl.program_id(1) == 0))
    def _():
        def copy(i):
            return pltpu.make_async_copy(h_hbm.at[pl.ds(tok_ref[0, 0, i], 1), :],
                                         g_buf.at[pl.ds(i, 1), :], sem)

        def start(i, c):
            copy(i).start()
            return c

        def wait(i, c):
            copy(i).wait()
            return c

        lax.fori_loop(0, tm, start, 0)
        lax.fori_loop(0, tm, wait, 0)
        hb_ref[...] = g_buf[...].astype(BF16)

    @pl.when(live)
    def _():
        _glu_kernel(hb_ref, w1_ref, w3_ref, o_ref)

    @pl.when(jnp.logical_not(live))
    def _():
        o_ref[...] = jnp.zeros_like(o_ref)


def _idle_pinned(idx, n_blocks, last_axis):
    def index_map(r, j, te, nu):
        blk = jnp.where(r < nu[0], j, n_blocks - 1)
        return (idx, te[r], 0, blk) if last_axis else (idx, te[r], blk, 0)
    return index_map


def grouped_glu(h32, row_token, tile_expert, n_used, w1, w3, idx, tm, tn):
    d = h32.shape[1]
    p = row_token.shape[0]
    nr = p // tm
    dff = w1.shape[-1]
    nj = dff // tn
    w_spec = pl.BlockSpec((None, None, d, tn), _idle_pinned(idx, nj, True))
    grid_spec = pltpu.PrefetchScalarGridSpec(
        num_scalar_prefetch=2,
        grid=(nr, nj),
        in_specs=[pl.BlockSpec((1, 1, tm), lambda r, j, te, nu: (r, 0, 0), memory_space=pltpu.SMEM),
                  ANY_SPEC, w_spec, w_spec],
        out_specs=pl.BlockSpec((tm, tn), lambda r, j, te, nu: (r, j)),
        scratch_shapes=[pltpu.VMEM((tm, d), F32), pltpu.VMEM((tm, d), BF16),
                        pltpu.SemaphoreType.DMA],
    )
    return pl.pallas_call(
        functools.partial(_grouped_glu_kernel, tm=tm),
        grid_spec=grid_spec,
        out_shape=jax.ShapeDtypeStruct((p, dff), BF16),
        compiler_params=_cparams("arbitrary", "arbitrary"),
        name="grouped_glu",
    )(tile_expert, n_used, row_token.reshape(nr, 1, tm), h32, w1, w3)


def _grouped_down_kernel(te_ref, nu_ref, a_ref, w2_ref, o_ref):
    live = pl.program_id(0) < nu_ref[0]

    @pl.when(live)
    def _():
        o_ref[...] = _dot(a_ref[...], w2_ref[...].astype(BF16))

    @pl.when(jnp.logical_not(live))
    def _():
        o_ref[...] = jnp.zeros_like(o_ref)


def grouped_down(act, tile_expert, n_used, w2, idx, tm, tn):
    p, dff = act.shape
    d = w2.shape[-1]
    nj = d // tn
    grid_spec = pltpu.PrefetchScalarGridSpec(
        num_scalar_prefetch=2,
        grid=(p // tm, nj),
        in_specs=[pl.BlockSpec((tm, dff), lambda r, j, te, nu: (r, 0)),
                  pl.BlockSpec((None, None, dff, tn), _idle_pinned(idx, nj, True))],
        out_specs=pl.BlockSpec((tm, tn), lambda r, j, te, nu: (r, j)),
    )
    return pl.pallas_call(
        _grouped_down_kernel,
        grid_spec=grid_spec,
        out_shape=jax.ShapeDtypeStruct((p, d), F32),
        compiler_params=_cparams("arbitrary", "arbitrary"),
        name="grouped_down",
    )(tile_expert, n_used, act, w2)


def _combine_kernel(d1_ref, d2_ref, x_ref, r_ref, g_ref, yg_hbm, yp_ref, ys_ref, y1_buf, y2_buf, sem,
                    *, tc, np_tiles):
    def copies(t):
        return (pltpu.make_async_copy(yg_hbm.at[pl.ds(d1_ref[0, 0, t], 1), :],
                                      y1_buf.at[pl.ds(t, 1), :], sem),
                pltpu.make_async_copy(yg_hbm.at[pl.ds(d2_ref[0, 0, t], 1), :],
                                      y2_buf.at[pl.ds(t, 1), :], sem))

    def start(t, c):
        for cp in copies(t):
            cp.start()
        return c

    def wait(t, c):
        for cp in copies(t):
            cp.wait()
        return c

    lax.fori_loop(0, tc, start, 0)
    lax.fori_loop(0, tc, wait, 0)
    route = r_ref[...]
    i1 = route[:, 0:1]
    i2 = route[:, 1:2]
    w1 = route[:, 2:3]
    w2 = route[:, 3:4]
    first_is_low = i1 < i2
    ya = jnp.where(first_is_low, w1 * y1_buf[...], w2 * y2_buf[...])
    yb = jnp.where(first_is_low, w2 * y2_buf[...], w1 * y1_buf[...])
    y = _rmsnorm_rows(x_ref[...] + (ya + yb), g_ref[...])
    is_prompt = pl.program_id(0) < np_tiles

    @pl.when(is_prompt)
    def _():
        yp_ref[...] = y

    @pl.when(jnp.logical_not(is_prompt))
    def _():
        ys_ref[...] = y


def moe_combine_norm(x, route, yg, d1, d2, g, n_p, tc):
    n, d = x.shape
    nc = n // tc
    np_tiles = n_p // tc
    idx = pl.BlockSpec((1, 1, tc), lambda c: (c, 0, 0), memory_space=pltpu.SMEM)
    return pl.pallas_call(
        functools.partial(_combine_kernel, tc=tc, np_tiles=np_tiles),
        grid=(nc,),
        in_specs=[idx, idx, pl.BlockSpec((tc, d), lambda c: (c, 0)),
                  pl.BlockSpec((tc, ROUTER_LANES), lambda c: (c, 0)),
                  pl.BlockSpec((1, d), lambda c: (0, 0)), ANY_SPEC],
        out_specs=[pl.BlockSpec((tc, d), lambda c: (jnp.minimum(c, np_tiles - 1), 0)),
                   pl.BlockSpec((tc, d), lambda c: (jnp.maximum(c - np_tiles, 0), 0))],
        out_shape=[jax.ShapeDtypeStruct((n_p, d), F32), jax.ShapeDtypeStruct((n - n_p, d), F32)],
        scratch_shapes=[pltpu.VMEM((tc, d), F32), pltpu.VMEM((tc, d), F32),
                        pltpu.SemaphoreType.DMA],
        compiler_params=_cparams("arbitrary"),
        name="moe_combine_norm",
    )(d1.reshape(nc, 1, tc), d2.reshape(nc, 1, tc), x, route, g.reshape(1, d), yg)


def moe_plan(route, tm):
    n = route.shape[0]
    i1 = route[:, 0].astype(jnp.int32)
    i2 = route[:, 1].astype(jnp.int32)
    experts = jnp.arange(N_EXPERTS, dtype=jnp.int32)
    onehot = ((i1[:, None] == experts) | (i2[:, None] == experts)).astype(jnp.int32)
    counts = jnp.sum(onehot, axis=0)
    tiles = (counts + tm - 1) // tm
    tile_end = jnp.cumsum(tiles)
    row_start = (tile_end - tiles) * tm
    pos = jnp.cumsum(onehot, axis=0) - onehot
    dest = row_start[None, :] + pos
    d1 = jnp.sum(jnp.where(i1[:, None] == experts, dest, 0), axis=1)
    d2 = jnp.sum(jnp.where(i2[:, None] == experts, dest, 0), axis=1)
    n_tiles = (2 * n) // tm + N_EXPERTS
    r = jnp.arange(n_tiles, dtype=jnp.int32)
    tile_expert = jnp.minimum(jnp.sum((r[:, None] >= tile_end[None, :]).astype(jnp.int32), axis=1),
                              N_EXPERTS - 1)
    n_used = tile_end[-1:].astype(jnp.int32)
    last_expert = jnp.sum(jnp.where(r == n_used[0] - 1, tile_expert, 0))
    tile_expert = jnp.where(r < n_used[0], tile_expert, last_expert).astype(jnp.int32)
    tokens = jnp.arange(n, dtype=jnp.int32)
    row_token = jnp.zeros((n_tiles * tm,), jnp.int32).at[jnp.concatenate([d1, d2])].set(
        jnp.concatenate([tokens, tokens]), unique_indices=True)
    return d1, d2, row_token, tile_expert, n_used


def _hgrn_kernel(p_ref, lb_ref, hn_ref, s0_ref, out_hbm, oa_ref, sout_ref,
                 st_ref, q_s, k_s, v_s, b_s, o_s, *, tt):
    del out_hbm
    c_len = HGRN_CHUNK
    t = pl.program_id(1)

    @pl.when(t == 0)
    def _():
        for h in range(N_HEADS):
            st_ref[h] = s0_ref[0, h].T

    lb = lb_ref[...]
    keep = 1.0 - lb
    z = p_ref[:, D_BRANCH:2 * D_BRANCH]
    e = jnp.exp(-jnp.abs(z))
    r = 1.0 / (1.0 + e)
    er = e * r
    logf = jnp.log(jnp.maximum(lb, LB_FLOOR) + keep * jnp.where(z >= 0.0, r, er))
    k_s[...] = keep * jnp.where(z >= 0.0, er, r)
    q_s[...] = _silu(p_ref[:, 0:D_BRANCH])
    v_s[...] = p_ref[:, 2 * D_BRANCH:3 * D_BRANCH]
    row = lax.broadcasted_iota(jnp.int32, (tt, tt), 0)
    col = lax.broadcasted_iota(jnp.int32, (tt, tt), 1)
    tri = jnp.where(((row ^ col) < c_len) & (col <= row), 1.0, 0.0).astype(BF16)
    lf_hi, lf_lo = _split_bf16(logf * LOG2_E)
    b_s[...] = _dot(tri, lf_hi) + _dot(tri, lf_lo)

    ones = jnp.ones((D_HEAD, D_HEAD), BF16)
    row_c = lax.broadcasted_iota(jnp.int32, (c_len, 1), 0)

    def chunk(c, carry):
        r0 = pl.multiple_of(c * c_len, c_len)
        qc = q_s[pl.ds(r0, c_len), :]
        kc = k_s[pl.ds(r0, c_len), :]
        vc = v_s[pl.ds(r0, c_len), :]
        bc = b_s[pl.ds(r0, c_len), :]
        b_last = bc[c_len - 1:c_len, :]
        qt = (qc * jnp.exp2(bc)).astype(BF16)
        kt = (kc * jnp.exp2(b_last - bc)).astype(BF16)
        decay = jnp.exp2(b_last)
        vb = vc.astype(BF16)
        xs = []
        for s in range(c_len):
            diff = jnp.where(row_c >= s, bc - bc[s:s + 1, :], NEG_BIG)
            xs.append((qc * kc[s:s + 1, :] * jnp.exp2(diff)).astype(BF16))
        x_all = jnp.concatenate(xs, axis=0)
        heads = [slice(h * D_HEAD, (h + 1) * D_HEAD) for h in range(N_HEADS)]
        states = [st_ref[h] for h in range(N_HEADS)]
        carried = [_dot_nt(qt[:, hs], st.astype(BF16)) for hs, st in zip(heads, states)]
        updates = [_dot_tn(vb[:, hs], kt[:, hs]) for hs in heads]
        sums = [_dot(x_all[:, hs], ones) for hs in heads]
        outs = []
        for hs, acc, y in zip(heads, carried, sums):
            for s in range(c_len):
                acc = acc + y[s * c_len:(s + 1) * c_len, :] * vc[s:s + 1, hs]
            outs.append(acc)
        for h, (hs, st, up) in enumerate(zip(heads, states, updates)):
            st_ref[h] = st * decay[:, hs] + up
        o_s[pl.ds(r0, c_len), :] = jnp.concatenate(outs, axis=1)
        return carry

    lax.fori_loop(0, tt // c_len, chunk, 0, unroll=HGRN_UNROLL)

    gate = _silu(p_ref[:, 3 * D_BRANCH:4 * D_BRANCH])
    hn = hn_ref[...]
    for h in range(N_HEADS):
        hs = slice(h * D_HEAD, (h + 1) * D_HEAD)
        oh = o_s[:, hs]
        oh = oh * lax.rsqrt(jnp.mean(oh * oh, axis=-1, keepdims=True) + EPS) * hn
        oa_ref[:, hs] = (oh * gate[:, hs]).astype(oa_ref.dtype)

    @pl.when(t == pl.num_programs(1) - 1)
    def _():
        for h in range(N_HEADS):
            sout_ref[0, h] = st_ref[h].T


def branch_buffer(n_rows):
    return jnp.zeros((n_rows, D_BRANCH), BF16)


def hgrn_mixer(proj, lb, hnorm, s0, row0, n_seq, seq_len, tt, out):
    nt = seq_len // tt
    rb0 = row0 // tt
    scr = [pltpu.VMEM((N_HEADS, D_HEAD, D_HEAD), F32)] + [pltpu.VMEM((tt, D_BRANCH), F32)] * 5
    return pl.pallas_call(
        functools.partial(_hgrn_kernel, tt=tt),
        grid=(n_seq, nt),
        in_specs=[pl.BlockSpec((tt, 4 * D_BRANCH), lambda s, t: (rb0 + s * nt + t, 0)),
                  pl.BlockSpec((1, D_BRANCH), lambda s, t: (0, 0)),
                  pl.BlockSpec((1, D_HEAD), lambda s, t: (0, 0)),
                  pl.BlockSpec((1, N_HEADS, D_HEAD, D_HEAD), lambda s, t: (s, 0, 0, 0)), ANY_SPEC],
        out_specs=[pl.BlockSpec((tt, D_BRANCH), lambda s, t: (rb0 + s * nt + t, 0)),
                   pl.BlockSpec((1, N_HEADS, D_HEAD, D_HEAD), lambda s, t: (s, 0, 0, 0))],
        out_shape=[jax.ShapeDtypeStruct(out.shape, out.dtype),
                   jax.ShapeDtypeStruct((n_seq, N_HEADS, D_HEAD, D_HEAD), F32)],
        scratch_shapes=scr,
        input_output_aliases={4: 0},
        compiler_params=_cparams("arbitrary", "arbitrary"),
        name="hgrn_mixer",
    )(proj, lb.reshape(1, D_BRANCH), hnorm.reshape(1, D_HEAD), s0, out)


def _pool_conv_kernel(p_ref, hp_ref, hc_ref, wp_ref, ps_ref, cw_ref, ob_hbm, oc_hbm,
                      ob_ref, oc_ref, pn_ref, cn_ref, xe, ue, *, tt, pos0):
    del ob_hbm, oc_hbm
    t = pl.program_id(1)

    @pl.when(t == 0)
    def _():
        xe[0:POOL_HALO, :] = hp_ref[0]
        ue[0:CONV_HALO, :] = hc_ref[0]

    @pl.when(t > 0)
    def _():
        xe[0:POOL_HALO, :] = xe[tt:tt + POOL_HALO, :]
        ue[0:CONV_HALO, :] = ue[tt:tt + CONV_HALO, :]

    x = p_ref[:, 0:D_BRANCH]
    xe[POOL_HALO:POOL_HALO + tt, :] = x
    u = p_ref[:, 3 * D_BRANCH:4 * D_BRANCH] * p_ref[:, D_BRANCH:2 * D_BRANCH]
    ue[CONV_HALO:CONV_HALO + tt, :] = u

    pos = pos0 + t * tt + lax.broadcasted_iota(jnp.int32, (tt, 1), 0)
    group = D_BRANCH // len(POOL_WINDOWS)
    for g, w in enumerate(POOL_WINDOWS):
        ls = slice(g * group, (g + 1) * group)
        s = x[:, ls]
        for j in range(1, w):
            s = s + xe[POOL_HALO - j:POOL_HALO - j + tt, ls]
        cnt = jnp.minimum(pos + 1, w).astype(F32)
        dlt = s / cnt - x[:, ls]
        y = _dot(dlt.astype(BF16), wp_ref[g].astype(BF16))
        ob_ref[:, ls] = (y * ps_ref[:, ls]).astype(ob_ref.dtype)

    y = ue[CONV_HALO - 2:CONV_HALO - 2 + tt, :] * cw_ref[0:1, :]
    for j in range(1, CONV_W):
        y = y + ue[CONV_HALO - 2 + j:CONV_HALO - 2 + j + tt, :] * cw_ref[j:j + 1, :]
    oc_ref[...] = (p_ref[:, 2 * D_BRANCH:3 * D_BRANCH] * y).astype(oc_ref.dtype)

    pn_ref[0] = xe[tt:tt + POOL_HALO, :]
    cn_ref[0] = ue[tt:tt + CONV_HALO, :]


def pool_conv_mixer(proj, hist_pool, hist_conv, w_pool, pool_scale, conv_w,
                    row0, n_seq, seq_len, tt, pos0, outs):
    nt = seq_len // tt
    rb0 = row0 // tt
    out_shape = jax.ShapeDtypeStruct(outs[0].shape, outs[0].dtype)
    cw = jnp.zeros((8, D_BRANCH), F32).at[:CONV_W].set(conv_w)
    out_row = pl.BlockSpec((tt, D_BRANCH), lambda s, t: (rb0 + s * nt + t, 0))
    return pl.pallas_call(
        functools.partial(_pool_conv_kernel, tt=tt, pos0=pos0),
        grid=(n_seq, nt),
        in_specs=[pl.BlockSpec((tt, 4 * D_BRANCH), lambda s, t: (rb0 + s * nt + t, 1)),
                  pl.BlockSpec((1, POOL_HALO, D_BRANCH), lambda s, t: (s, 0, 0)),
                  pl.BlockSpec((1, CONV_HALO, D_BRANCH), lambda s, t: (s, 0, 0)),
                  pl.BlockSpec(w_pool.shape, lambda s, t: (0, 0, 0)),
                  pl.BlockSpec((1, D_BRANCH), lambda s, t: (0, 0)),
                  pl.BlockSpec((8, D_BRANCH), lambda s, t: (0, 0)), ANY_SPEC, ANY_SPEC],
        out_specs=[out_row, out_row,
                   pl.BlockSpec((1, POOL_HALO, D_BRANCH), lambda s, t: (s, 0, 0)),
                   pl.BlockSpec((1, CONV_HALO, D_BRANCH), lambda s, t: (s, 0, 0))],
        out_shape=[out_shape, out_shape,
                   jax.ShapeDtypeStruct((n_seq, POOL_HALO, D_BRANCH), F32),
                   jax.ShapeDtypeStruct((n_seq, CONV_HALO, D_BRANCH), F32)],
        scratch_shapes=[pltpu.VMEM((tt + POOL_HALO, D_BRANCH), F32),
                        pltpu.VMEM((tt + CONV_HALO, D_BRANCH), F32)],
        input_output_aliases={6: 0, 7: 1},
        compiler_params=_cparams("arbitrary", "arbitrary"),
        name="pool_conv_mixer",
    )(proj, hist_pool, hist_conv, w_pool, pool_scale.reshape(1, D_BRANCH), cw, *outs)


def _sb_tile(q_ref, kv, o_acc, run, masked):
    tq = q_ref.shape[0]
    scale = D_HEAD ** -0.5
    kr = lax.broadcasted_iota(jnp.int32, (SB_TILE, 2 * SB_TILE), 0)
    kc = lax.broadcasted_iota(jnp.int32, (SB_TILE, 2 * SB_TILE), 1)
    sums = jnp.where((kc >= SB_TILE) | (kr > kc), 1.0, 0.0).astype(BF16)
    if masked:
        qi = lax.broadcasted_iota(jnp.int32, (tq, SB_TILE), 0)
        ki = lax.broadcasted_iota(jnp.int32, (tq, SB_TILE), 1)
        mask = ki < qi
    run_old = run[...]
    heads = [slice(h * D_HEAD, (h + 1) * D_HEAD) for h in range(N_HEADS)]
    kvs = [kv(h) for h in range(N_HEADS)]
    zs = [_dot_nt(q_ref[:, hs], k.astype(BF16)) * scale for hs, (k, _) in zip(heads, kvs)]
    lks = []
    for z in zs:
        log_keep = -(jnp.maximum(z, 0.0) + jnp.log1p(jnp.exp(-jnp.abs(z))))
        lks.append(jnp.where(mask, log_keep, 0.0) if masked else log_keep)
    splits = [_split_bf16(lk) for lk in lks]
    sms = [_dot(hi, sums) + _dot(lo, sums) for hi, lo in splits]
    probs = []
    for hs, z, lk, sm in zip(heads, zs, lks, sms):
        a = jnp.exp(z + lk + sm[:, 0:SB_TILE] + run_old[:, hs])
        probs.append((jnp.where(mask, a, 0.0) if masked else a).astype(BF16))
    o_new = [_dot(a, v.astype(BF16)) for a, (_, v) in zip(probs, kvs)]
    o_acc[...] += jnp.concatenate(o_new, axis=1)
    run[...] = run_old + jnp.concatenate([sm[:, SB_TILE:2 * SB_TILE] for sm in sms], axis=1)


def _sb_alive(run):
    return (jnp.max(run[...]) > SB_DONE).astype(jnp.int32)


def _sb_past_loop(q_ref, kv_tile, o_acc, run, n_tiles):
    def cond(c):
        j, alive = c
        return (j >= 0) & (alive > 0)

    def body(c):
        j, _ = c
        r0 = pl.multiple_of(j * SB_TILE, SB_TILE)
        _sb_tile(q_ref, functools.partial(kv_tile, r0), o_acc, run, masked=False)
        return j - 1, _sb_alive(run)

    lax.while_loop(cond, body, (jnp.int32(n_tiles) - 1, _sb_alive(run)))


def _sb_prompt_kernel(q_ref, k_ref, v_ref, out_hbm, o_ref, o_acc, run):
    del out_hbm
    i = pl.program_id(1)
    o_acc[...] = jnp.zeros_like(o_acc)
    run[...] = jnp.zeros_like(run)

    def kv_tile(r0, h):
        hs = slice(h * D_HEAD, (h + 1) * D_HEAD)
        return k_ref[pl.ds(r0, SB_TILE), hs], v_ref[pl.ds(r0, SB_TILE), hs]

    _sb_tile(q_ref, functools.partial(kv_tile, pl.multiple_of(i * SB_TILE, SB_TILE)), o_acc, run,
             masked=True)
    _sb_past_loop(q_ref, kv_tile, o_acc, run, i)
    o_ref[...] = o_acc[...].astype(o_ref.dtype)


def sb_prompt_mixer(qkv, n_seq, seq_len, out):
    nq = seq_len // SB_TILE
    seq = lambda c: pl.BlockSpec((seq_len, D_BRANCH), lambda s, i: (s, c))
    return pl.pallas_call(
        _sb_prompt_kernel,
        grid=(n_seq, nq),
        in_specs=[pl.BlockSpec((SB_TILE, D_BRANCH), lambda s, i: (s * nq + i, 0)), seq(1), seq(2),
                  ANY_SPEC],
        out_specs=pl.BlockSpec((SB_TILE, D_BRANCH), lambda s, i: (s * nq + i, 0)),
        out_shape=jax.ShapeDtypeStruct(out.shape, out.dtype),
        scratch_shapes=[pltpu.VMEM((SB_TILE, D_BRANCH), F32), pltpu.VMEM((SB_TILE, D_BRANCH), F32)],
        input_output_aliases={3: 0},
        compiler_params=_cparams("arbitrary", "arbitrary"),
        name="sb_prompt_mixer",
    )(qkv, qkv, qkv, out)


def _sb_sample_kernel(qkv_ref, kp_ref, vp_ref, out_hbm, o_ref, o_acc, run, *, n_past_tiles):
    del out_hbm
    o_acc[...] = jnp.zeros_like(o_acc)
    run[...] = jnp.zeros_like(run)
    tq = qkv_ref.shape[0]
    q_ref = qkv_ref.at[:, 0:D_BRANCH]
    pad = jnp.zeros((SB_TILE - tq, D_BRANCH), qkv_ref.dtype)
    k_new = jnp.concatenate([qkv_ref[:, D_BRANCH:2 * D_BRANCH], pad], axis=0)
    v_new = jnp.concatenate([qkv_ref[:, 2 * D_BRANCH:3 * D_BRANCH], pad], axis=0)

    def kv_new(h):
        hs = slice(h * D_HEAD, (h + 1) * D_HEAD)
        return k_new[:, hs], v_new[:, hs]

    def kv_past(r0, h):
        return kp_ref[pl.ds(r0, SB_TILE), h, :], vp_ref[pl.ds(r0, SB_TILE), h, :]

    _sb_tile(q_ref, kv_new, o_acc, run, masked=True)
    _sb_past_loop(q_ref, kv_past, o_acc, run, n_past_tiles)
    o_ref[...] = o_acc[...].astype(o_ref.dtype)


def sb_sample_mixer(qkv, k_cache, v_cache, layer, row0, n_seq, seq_len, out):
    past = k_cache.shape[2]
    rb0 = row0 // seq_len
    cache = pl.BlockSpec((None, None, past, N_HEADS, D_HEAD), lambda s: (layer, s, 0, 0, 0))
    return pl.pallas_call(
        functools.partial(_sb_sample_kernel, n_past_tiles=past // SB_TILE),
        grid=(n_seq,),
        in_specs=[pl.BlockSpec((seq_len, 3 * D_BRANCH), lambda s: (rb0 + s, 0)), cache, cache, ANY_SPEC],
        out_specs=pl.BlockSpec((seq_len, D_BRANCH), lambda s: (rb0 + s, 0)),
        out_shape=jax.ShapeDtypeStruct(out.shape, out.dtype),
        scratch_shapes=[pltpu.VMEM((seq_len, D_BRANCH), F32), pltpu.VMEM((seq_len, D_BRANCH), F32)],
        input_output_aliases={3: 0},
        compiler_params=_cparams("arbitrary"),
        name="sb_sample_mixer",
    )(qkv, k_cache, v_cache, out)


def _tile(n, want):
    if n <= want:
        return n
    for t in range(want, 7, -8):
        if n % t == 0:
            return t
    return n


def kernel(x_prompt, x_sample, state_hgrn, state_pool, state_conv, cache_k, cache_v, norm_mix, norm_ffn, norm_final, w_in, hgrn_lower_bound, hgrn_out_norm, pool_w, pool_scale, conv_w, w_branch, w_gate, b_gate, w_out, ffn_w1, ffn_w3, ffn_w2, moe_router, moe_w1, moe_w3, moe_w2):
    depth = w_in.shape[0]
    pb, pt, d = x_prompt.shape
    sb, st, _ = x_sample.shape
    past = cache_k.shape[2]
    n_p = pb * pt
    n_s = sb * st
    n = n_p + n_s
    assert depth % 2 == 0, "the last layer is expected to be a MoE layer"
    x = jnp.concatenate([x_prompt.reshape(n_p, d), x_sample.reshape(n_s, d)], axis=0)

    tm_row = _tile(n, 512)
    tm_mm = _tile(n_s, 2048)
    tm_moe = _tile(n, 1024)
    tc = _tile(n_s, 256)
    tt_p = _tile(pt, 256)
    assert n_p % tm_mm == 0 and n_p % tc == 0

    sm = jax.nn.softmax(hgrn_lower_bound.astype(F32), axis=0)
    lbs = jnp.cumsum(sm, axis=0) - sm[0:1]

    zeros_state = jnp.zeros((pb, N_HEADS, D_HEAD, D_HEAD), F32)
    zeros_pool = jnp.zeros((pb, POOL_HALO, D_BRANCH), F32)
    zeros_conv = jnp.zeros((pb, CONV_HALO, D_BRANCH), F32)
    pool_pad = ((0, 0), (0, 0), (POOL_HALO - state_pool.shape[2], 0), (0, 0))
    conv_pad = ((0, 0), (0, 0), (CONV_HALO - state_conv.shape[2], 0), (0, 0))
    sample_pool = jnp.pad(state_pool, pool_pad)
    sample_conv = jnp.pad(state_conv, conv_pad)

    outs = {k: [] for k in ("p_h", "p_pool", "p_conv", "s_h", "s_pool", "s_conv")}
    caches = cache_buffers(depth, n_p, n_s)
    xn = rmsnorm(x, norm_mix[0], BF16, tm_row)
    y_p = y_s = None
    for l in range(depth):
        proj = in_proj(xn, w_in, l, 8 * D_BRANCH, tm_mm, D_BRANCH)
        qkv16, *caches = qkv_proj(xn, w_in, l, 8 * D_BRANCH, n_p, caches, tm_mm // 2)

        oa, hs_p = hgrn_mixer(proj, lbs[l], hgrn_out_norm[l], zeros_state, 0, pb, pt, tt_p,
                              branch_buffer(n))
        oa, hs_s = hgrn_mixer(proj, lbs[l], hgrn_out_norm[l], state_hgrn[l], n_p, sb, st, st, oa)
        ob, oc, pool_p, conv_p = pool_conv_mixer(
            proj, zeros_pool, zeros_conv, pool_w[l], pool_scale[l], conv_w[l], 0, pb, pt, tt_p, 0,
            (branch_buffer(n), branch_buffer(n)))
        ob, oc, pool_s, conv_s = pool_conv_mixer(
            proj, sample_pool[l], sample_conv[l], pool_w[l], pool_scale[l], conv_w[l],
            n_p, sb, st, st, past, (ob, oc))
        od = sb_prompt_mixer(qkv16, pb, pt, branch_buffer(n))
        od = sb_sample_mixer(qkv16, cache_k, cache_v, l, n_p, sb, st, od)

        merged = gate_merge(xn, (oa, ob, oc, od), w_gate, b_gate, w_branch, l, tm_moe, 256)
        x = out_proj(merged, w_out, l, x, tm_mm, D_BRANCH)

        outs["p_h"].append(hs_p)
        outs["s_h"].append(hs_s)
        outs["p_pool"].append(pool_p[:, 1:])
        outs["s_pool"].append(pool_s[:, 1:])
        outs["p_conv"].append(conv_p[:, CONV_HALO - (CONV_W - 1):])
        outs["s_conv"].append(conv_s[:, CONV_HALO - (CONV_W - 1):])

        if l % 2 == 0:
            h = rmsnorm(x, norm_ffn[l], BF16, tm_row)
            act = dense_glu(h, ffn_w1, ffn_w3, l // 2, tm_mm, D_BRANCH)
            x = out_proj(act, ffn_w2, l // 2, x, tm_moe, 256)
            xn = rmsnorm(x, norm_mix[l + 1], BF16, tm_row)
        else:
            h32, route = rmsnorm_router(x, norm_ffn[l], moe_router[l // 2], tm_row)
            d1, d2, row_token, tile_expert, n_used = moe_plan(route, tm_moe)
            act = grouped_glu(h32, row_token, tile_expert, n_used, moe_w1, moe_w3, l // 2,
                              tm_moe, D_BRANCH)
            yg = grouped_down(act, tile_expert, n_used, moe_w2, l // 2, tm_moe, 256)
            if l + 1 < depth:
                raise NotImplementedError("a MoE layer followed by another layer")
            y_p, y_s = moe_combine_norm(x, route, yg, d1, d2, norm_final, n_p, tc)

    pk, pv, sk, sv = caches
    st_ = lambda k: jnp.stack(outs[k])
    return (y_p.reshape(pb, pt, d), y_s.reshape(sb, st, d),
            st_("p_h"), st_("p_pool"), st_("p_conv"),
            pk.reshape(depth, pb, pt, N_HEADS, D_HEAD), pv.reshape(depth, pb, pt, N_HEADS, D_HEAD),
            st_("s_h"), st_("s_pool"), st_("s_conv"),
            sk.reshape(depth, sb, st, N_HEADS, D_HEAD), sv.reshape(depth, sb, st, N_HEADS, D_HEAD))
```

```python
import functools

import jax
import jax.numpy as jnp
from jax import lax
from jax.experimental import pallas as pl
from jax.experimental.pallas import tpu as pltpu

F32 = jnp.float32
BF16 = jnp.bfloat16

EPS = 1e-6
NEG_BIG = -1e30
LB_FLOOR = 1e-30
LOG2_E = 1.4426950408889634
N_HEADS = 4
D_HEAD = 128
D_BRANCH = N_HEADS * D_HEAD
POOL_WINDOWS = (2, 4, 8, 16)
POOL_HALO = 16
CONV_W = 3
CONV_HALO = 8
N_EXPERTS = 8
HGRN_CHUNK = 16
HGRN_UNROLL = 4
SB_TILE = 128
SB_DONE = -120.0
ROUTER_LANES = 128
MOE_TILE = 2048
MOE_SUBS = 4
DMA_UNROLL = 8

V7X_VMEM_LIMIT_BYTES = 56 * 1024 * 1024

ANY_SPEC = pl.BlockSpec(memory_space=pl.ANY)


def _cparams(*sem):
    return pltpu.CompilerParams(dimension_semantics=sem,
                                vmem_limit_bytes=V7X_VMEM_LIMIT_BYTES)


def _split_bf16(x):
    hi = x.astype(BF16)
    lo = (x - hi.astype(F32)).astype(BF16)
    return hi, lo


def _dot(a, b):
    return jnp.dot(a, b, preferred_element_type=F32)


def _dot_nt(a, b):
    return lax.dot_general(a, b, (((1,), (1,)), ((), ())), preferred_element_type=F32)


def _dot_tn(a, b):
    return lax.dot_general(a, b, (((0,), (0,)), ((), ())), preferred_element_type=F32)


def _silu(a):
    return a * jax.nn.sigmoid(a)


def _rmsnorm_rows(x, g):
    return x * lax.rsqrt(jnp.mean(x * x, axis=-1, keepdims=True) + EPS) * g


def _two_source_specs(xs, tm, width, col_of):
    def index_map(i, *j, t0, nt):
        inside = (i >= t0) & (i < t0 + nt)
        return jnp.clip(i - t0, 0, nt - 1), jnp.where(inside, col_of(*j), 0)

    starts, specs, t0 = [], [], 0
    for x in xs:
        nt = x.shape[0] // tm
        starts.append(t0)
        specs.append(pl.BlockSpec((tm, width), functools.partial(index_map, t0=t0, nt=nt)))
        t0 += nt
    return specs, starts, t0


def _pick_source(refs, starts):
    i = pl.program_id(0)
    x = refs[0][...]
    for ref, t0 in zip(refs[1:], starts[1:]):
        x = jnp.where(i >= t0, ref[...], x)
    return x


def _rms_kernel(*refs, starts):
    g_ref, o_ref = refs[-2:]
    x = _pick_source(refs[:-2], starts)
    o_ref[...] = _rmsnorm_rows(x, g_ref[...]).astype(o_ref.dtype)


def rmsnorm(xs, g, out_dtype, tm):
    xs = xs if isinstance(xs, tuple) else (xs,)
    d = xs[0].shape[1]
    specs, starts, n_tiles = _two_source_specs(xs, tm, d, lambda: 0)
    return pl.pallas_call(
        functools.partial(_rms_kernel, starts=starts),
        grid=(n_tiles,),
        in_specs=specs + [pl.BlockSpec((1, d), lambda i: (0, 0))],
        out_specs=pl.BlockSpec((tm, d), lambda i: (i, 0)),
        out_shape=jax.ShapeDtypeStruct((n_tiles * tm, d), out_dtype),
        compiler_params=_cparams("arbitrary"),
        name="rmsnorm",
    )(*xs, g.reshape(1, d))


def _rms_router_kernel(x_ref, g_ref, r_ref, h32_ref, route_ref):
    h = _rmsnorm_rows(x_ref[...], g_ref[...])
    h_hi, h_lo = _split_bf16(h)
    h32_ref[...] = h_hi.astype(F32)
    r_hi, r_lo = _split_bf16(r_ref[...])
    logits = _dot(h_hi, r_hi) + (_dot(h_lo, r_hi) + _dot(h_hi, r_lo))
    lane = lax.broadcasted_iota(jnp.int32, logits.shape, 1).astype(F32)
    neg_inf = jnp.float32(-jnp.inf)
    l1 = jnp.where(lane < N_EXPERTS, logits, neg_inf)
    m1 = jnp.max(l1, axis=-1, keepdims=True)
    i1 = jnp.min(jnp.where(l1 == m1, lane, float(ROUTER_LANES)), axis=-1, keepdims=True)
    l2 = jnp.where(lane == i1, neg_inf, l1)
    m2 = jnp.max(l2, axis=-1, keepdims=True)
    i2 = jnp.min(jnp.where(l2 == m2, lane, float(ROUTER_LANES)), axis=-1, keepdims=True)
    e = jnp.exp(m2 - m1)
    w1 = 1.0 / (1.0 + e)
    w2 = e / (1.0 + e)
    route_ref[...] = jnp.where(lane == 0.0, i1, jnp.where(lane == 1.0, i2, jnp.where(
        lane == 2.0, w1, jnp.where(lane == 3.0, w2, 0.0))))


def rmsnorm_router(x, g, router, tm):
    n, d = x.shape
    r_pad = jnp.zeros((d, ROUTER_LANES), F32).at[:, :N_EXPERTS].set(router)
    row = pl.BlockSpec((tm, d), lambda i: (i, 0))
    return pl.pallas_call(
        _rms_router_kernel,
        grid=(n // tm,),
        in_specs=[row, pl.BlockSpec((1, d), lambda i: (0, 0)),
                  pl.BlockSpec((d, ROUTER_LANES), lambda i: (0, 0))],
        out_specs=[row, pl.BlockSpec((tm, ROUTER_LANES), lambda i: (i, 0))],
        out_shape=[jax.ShapeDtypeStruct((n, d), F32), jax.ShapeDtypeStruct((n, ROUTER_LANES), F32)],
        compiler_params=_cparams("parallel"),
        name="rmsnorm_router",
    )(x, g.reshape(1, d), r_pad)


def _proj_kernel(x_ref, w_ref, o_ref):
    o_ref[...] = _dot(x_ref[...], w_ref[...].astype(BF16)).astype(o_ref.dtype)


def in_proj(xn, w_all, layer, n_cols, tm, tn):
    n, k = xn.shape
    return pl.pallas_call(
        _proj_kernel,
        grid=(n // tm, n_cols // tn),
        in_specs=[pl.BlockSpec((tm, k), lambda i, j: (i, 0)),
                  pl.BlockSpec((None, k, tn), lambda i, j: (layer, 0, j))],
        out_specs=pl.BlockSpec((tm, tn), lambda i, j: (i, j)),
        out_shape=jax.ShapeDtypeStruct((n, n_cols), F32),
        compiler_params=_cparams("parallel", "arbitrary"),
        name="in_proj",
    )(xn, w_all)


def _qkv_kernel(x_ref, w_ref, pk_hbm, pv_hbm, sk_hbm, sv_hbm,
                o16_ref, pk_ref, pv_ref, sk_ref, sv_ref, *, np_tiles):
    del pk_hbm, pv_hbm, sk_hbm, sv_hbm
    r = _dot(x_ref[...], w_ref[...].astype(BF16))
    o16_ref[...] = r.astype(o16_ref.dtype)
    is_prompt = pl.program_id(0) < np_tiles
    j = pl.program_id(1)

    def put(dst):
        for h in range(N_HEADS):
            dst[:, h, :] = r[:, h * D_HEAD:(h + 1) * D_HEAD]

    for col, p_ref, s_ref in ((1, pk_ref, sk_ref), (2, pv_ref, sv_ref)):
        pl.when((j == col) & is_prompt)(functools.partial(put, p_ref))
        pl.when((j == col) & jnp.logical_not(is_prompt))(functools.partial(put, s_ref))


def cache_buffers(depth, n_p, n_s):
    return [jnp.zeros((depth * n, N_HEADS, D_HEAD), F32) for n in (n_p, n_p, n_s, n_s)]


def qkv_proj(xn, w_all, layer, col_start, n_p, caches, tm):
    n, k = xn.shape
    n_s = n - n_p
    np_tiles, ns_tiles = n_p // tm, n_s // tm
    cb0 = col_start // D_BRANCH
    head_blk = (tm, N_HEADS, D_HEAD)
    p_spec = pl.BlockSpec(head_blk, lambda i, j: (layer * np_tiles + jnp.minimum(i, np_tiles - 1), 0, 0))
    s_spec = pl.BlockSpec(head_blk, lambda i, j: (layer * ns_tiles + jnp.maximum(i - np_tiles, 0), 0, 0))
    return pl.pallas_call(
        functools.partial(_qkv_kernel, np_tiles=np_tiles),
        grid=(n // tm, 3),
        in_specs=[pl.BlockSpec((tm, k), lambda i, j: (i, 0)),
                  pl.BlockSpec((None, k, D_BRANCH), lambda i, j: (layer, 0, cb0 + j))]
                 + [ANY_SPEC] * 4,
        out_specs=[pl.BlockSpec((tm, D_BRANCH), lambda i, j: (i, j)), p_spec, p_spec, s_spec, s_spec],
        out_shape=[jax.ShapeDtypeStruct((n, 3 * D_BRANCH), BF16)]
                  + [jax.ShapeDtypeStruct(c.shape, c.dtype) for c in caches],
        input_output_aliases={2: 1, 3: 2, 4: 3, 5: 4},
        compiler_params=_cparams("arbitrary", "arbitrary"),
        name="qkv_proj",
    )(xn, w_all, *caches)


def _out_proj_kernel(m_ref, w_ref, *refs, starts):
    o_ref = refs[-1]
    o_ref[...] = _pick_source(refs[:-1], starts) + _dot(m_ref[...], w_ref[...].astype(BF16))


def out_proj(a, w_all, layer, xs, tm, tn):
    xs = xs if isinstance(xs, tuple) else (xs,)
    n, k = a.shape
    d = xs[0].shape[1]
    specs, starts, n_tiles = _two_source_specs(xs, tm, tn, lambda j: j)
    assert n_tiles * tm == n
    return pl.pallas_call(
        functools.partial(_out_proj_kernel, starts=starts),
        grid=(n_tiles, d // tn),
        in_specs=[pl.BlockSpec((tm, k), lambda i, j: (i, 0)),
                  pl.BlockSpec((None, k, tn), lambda i, j: (layer, 0, j))] + specs,
        out_specs=pl.BlockSpec((tm, tn), lambda i, j: (i, j)),
        out_shape=jax.ShapeDtypeStruct((n, d), F32),
        compiler_params=_cparams("arbitrary", "arbitrary"),
        name="out_proj",
    )(a, w_all, *xs)


def _gate_merge_kernel(xn_ref, oa_ref, ob_ref, oc_ref, od_ref,
                       wg0_ref, wg1_ref, wg2_ref, wg3_ref,
                       bg0_ref, bg1_ref, bg2_ref, bg3_ref, wb_ref, o_ref):
    xn = xn_ref[...]
    branches = (oa_ref, ob_ref, oc_ref, od_ref)
    gates = (wg0_ref, wg1_ref, wg2_ref, wg3_ref)
    biases = (bg0_ref, bg1_ref, bg2_ref, bg3_ref)
    acc = None
    for n in range(4):
        g = jax.nn.sigmoid(_dot(xn, gates[n][...].astype(BF16)) + biases[n][...])
        p = _dot(branches[n][...], wb_ref[n].astype(BF16))
        acc = g * p if acc is None else acc + g * p
    o_ref[...] = acc.astype(o_ref.dtype)


def gate_merge(xn, branches, w_gate, b_gate, w_branch, layer, tm, tn):
    n, d = xn.shape
    nb = d // tn
    db = branches[0].shape[1]
    b3 = b_gate.reshape(b_gate.shape[0], 1, b_gate.shape[1])
    row = pl.BlockSpec((tm, d), lambda i, j: (i, 0))
    br = pl.BlockSpec((tm, db), lambda i, j: (i, 0))
    wg = [pl.BlockSpec((None, d, tn), functools.partial(lambda i, j, m: (layer, 0, m * nb + j), m=m))
          for m in range(4)]
    bg = [pl.BlockSpec((None, 1, tn), functools.partial(lambda i, j, m: (layer, 0, m * nb + j), m=m))
          for m in range(4)]
    wb = pl.BlockSpec((None, 4, db, tn), lambda i, j: (layer, 0, 0, j))
    return pl.pallas_call(
        _gate_merge_kernel,
        grid=(n // tm, nb),
        in_specs=[row, br, br, br, br] + wg + bg + [wb],
        out_specs=pl.BlockSpec((tm, tn), lambda i, j: (i, j)),
        out_shape=jax.ShapeDtypeStruct((n, d), BF16),
        compiler_params=_cparams("parallel", "arbitrary"),
        name="gate_merge",
    )(xn, *branches, w_gate, w_gate, w_gate, w_gate, b3, b3, b3, b3, w_branch)


def _glu_kernel(h_ref, w1_ref, w3_ref, o_ref):
    h = h_ref[...]
    a = _dot(h, w1_ref[...].astype(BF16))
    b = _dot(h, w3_ref[...].astype(BF16))
    o_ref[...] = (_silu(a) * b).astype(o_ref.dtype)


def dense_glu(h, w1, w3, idx, tm, tn):
    n, d = h.shape
    dff = w1.shape[-1]
    return pl.pallas_call(
        _glu_kernel,
        grid=(n // tm, dff // tn),
        in_specs=[pl.BlockSpec((tm, d), lambda i, j: (i, 0)),
                  pl.BlockSpec((None, d, tn), lambda i, j: (idx, 0, j)),
                  pl.BlockSpec((None, d, tn), lambda i, j: (idx, 0, j))],
        out_specs=pl.BlockSpec((tm, tn), lambda i, j: (i, j)),
        out_shape=jax.ShapeDtypeStruct((n, dff), BF16),
        compiler_params=_cparams("parallel", "arbitrary"),
        name="dense_glu",
    )(h, w1, w3)


def _row_dma_loop(trips, copies, start):
    def body(g, c):
        for u in range(DMA_UNROLL):
            for cp in copies(g * DMA_UNROLL + u):
                if start:
                    cp.start()
                else:
                    cp.wait()
        return c
    lax.fori_loop(0, trips, body, 0)


def _grouped_glu_kernel(te_ref, ns_ref, tok_ref, h_hbm, w1_ref, w3_ref, o_ref,
                        g_buf, hb_ref, sem, *, sub):
    del te_ref
    n_sub = ns_ref[pl.program_id(0)]
    g_rows = g_buf.shape[0]
    subs_per_pass = g_rows // sub

    for g in range(hb_ref.shape[0] // g_rows):
        live_here = jnp.clip(n_sub - g * subs_per_pass, 0, subs_per_pass)

        @pl.when((live_here > 0) & (pl.program_id(1) == 0))
        def _():
            def copies(i):
                return (pltpu.make_async_copy(h_hbm.at[pl.ds(tok_ref[0, 0, g * g_rows + i], 1), :],
                                              g_buf.at[pl.ds(i, 1), :], sem),)

            _row_dma_loop(live_here * (sub // DMA_UNROLL), copies, start=True)
            _row_dma_loop(live_here * (sub // DMA_UNROLL), copies, start=False)
            for m in range(subs_per_pass):
                @pl.when(m < live_here)
                def _():
                    hb_ref[pl.ds(g * g_rows + m * sub, sub), :] = g_buf[pl.ds(m * sub, sub), :].astype(BF16)

    w1 = w1_ref[...].astype(BF16)
    w3 = w3_ref[...].astype(BF16)
    for m in range(hb_ref.shape[0] // sub):
        rows = pl.ds(m * sub, sub)

        @pl.when(m < n_sub)
        def _():
            h = hb_ref[rows, :]
            o_ref[rows, :] = (_silu(_dot(h, w1)) * _dot(h, w3)).astype(o_ref.dtype)

        @pl.when(m >= n_sub)
        def _():
            o_ref[rows, :] = jnp.zeros((sub, o_ref.shape[1]), o_ref.dtype)


def _expert_block(idx, n_blocks, tiles_per_entry):
    def index_map(r, j, te, ns):
        e = r // tiles_per_entry
        live = ns[e] > (r % tiles_per_entry) * (MOE_SUBS // tiles_per_entry)
        return (idx, te[e], 0, jnp.where(live, j, n_blocks - 1))
    return index_map


def grouped_glu(h32, row_token, tile_expert, n_sub, w1, w3, idx, tn):
    d = h32.shape[1]
    p = row_token.shape[0]
    nr = p // MOE_TILE
    dff = w1.shape[-1]
    nj = dff // tn
    w_spec = pl.BlockSpec((None, None, d, tn), _expert_block(idx, nj, 1))
    grid_spec = pltpu.PrefetchScalarGridSpec(
        num_scalar_prefetch=2,
        grid=(nr, nj),
        in_specs=[pl.BlockSpec((1, 1, MOE_TILE), lambda r, j, te, ns: (r, 0, 0),
                               memory_space=pltpu.SMEM),
                  ANY_SPEC, w_spec, w_spec],
        out_specs=pl.BlockSpec((MOE_TILE, tn), lambda r, j, te, ns: (r, j)),
        scratch_shapes=[pltpu.VMEM((MOE_TILE // 2, d), F32), pltpu.VMEM((MOE_TILE, d), BF16),
                        pltpu.SemaphoreType.DMA],
    )
    return pl.pallas_call(
        functools.partial(_grouped_glu_kernel, sub=MOE_TILE // MOE_SUBS),
        grid_spec=grid_spec,
        out_shape=jax.ShapeDtypeStruct((p, dff), BF16),
        compiler_params=_cparams("arbitrary", "arbitrary"),
        name="grouped_glu",
    )(tile_expert, n_sub, row_token.reshape(nr, 1, MOE_TILE), h32, w1, w3)


def _grouped_down_kernel(te_ref, ns_ref, a_ref, w2_ref, o_ref, *, tiles_per_entry):
    del te_ref
    r = pl.program_id(0)
    live = ns_ref[r // tiles_per_entry] > (r % tiles_per_entry) * (MOE_SUBS // tiles_per_entry)

    @pl.when(live)
    def _():
        o_ref[...] = _dot(a_ref[...], w2_ref[...].astype(BF16))

    @pl.when(jnp.logical_not(live))
    def _():
        o_ref[...] = jnp.zeros_like(o_ref)


def grouped_down(act, tile_expert, n_sub, w2, idx, tm, tn):
    p, dff = act.shape
    d = w2.shape[-1]
    nj = d // tn
    per = MOE_TILE // tm
    grid_spec = pltpu.PrefetchScalarGridSpec(
        num_scalar_prefetch=2,
        grid=(p // tm, nj),
        in_specs=[pl.BlockSpec((tm, dff), lambda r, j, te, ns: (r, 0)),
                  pl.BlockSpec((None, None, dff, tn), _expert_block(idx, nj, per))],
        out_specs=pl.BlockSpec((tm, tn), lambda r, j, te, ns: (r, j)),
    )
    return pl.pallas_call(
        functools.partial(_grouped_down_kernel, tiles_per_entry=per),
        grid_spec=grid_spec,
        out_shape=jax.ShapeDtypeStruct((p, d), F32),
        compiler_params=_cparams("arbitrary", "arbitrary"),
        name="grouped_down",
    )(tile_expert, n_sub, act, w2)


def _combine_kernel(d1_ref, d2_ref, x_ref, r_ref, g_ref, yg_hbm, yp_ref, ys_ref, y1_buf, y2_buf, sem,
                    *, tc, np_tiles):
    def copies(t):
        return (pltpu.make_async_copy(yg_hbm.at[pl.ds(d1_ref[0, 0, t], 1), :],
                                      y1_buf.at[pl.ds(t, 1), :], sem),
                pltpu.make_async_copy(yg_hbm.at[pl.ds(d2_ref[0, 0, t], 1), :],
                                      y2_buf.at[pl.ds(t, 1), :], sem))

    _row_dma_loop(tc // DMA_UNROLL, copies, start=True)
    _row_dma_loop(tc // DMA_UNROLL, copies, start=False)
    route = r_ref[...]
    i1 = route[:, 0:1]
    i2 = route[:, 1:2]
    w1 = route[:, 2:3]
    w2 = route[:, 3:4]
    first_is_low = i1 < i2
    ya = jnp.where(first_is_low, w1 * y1_buf[...], w2 * y2_buf[...])
    yb = jnp.where(first_is_low, w2 * y2_buf[...], w1 * y1_buf[...])
    y = _rmsnorm_rows(x_ref[...] + (ya + yb), g_ref[...])
    is_prompt = pl.program_id(0) < np_tiles

    @pl.when(is_prompt)
    def _():
        yp_ref[...] = y

    @pl.when(jnp.logical_not(is_prompt))
    def _():
        ys_ref[...] = y


def moe_combine_norm(x, route, yg, d1, d2, g, n_p, tc):
    n, d = x.shape
    nc = n // tc
    np_tiles = n_p // tc
    idx = pl.BlockSpec((1, 1, tc), lambda c: (c, 0, 0), memory_space=pltpu.SMEM)
    return pl.pallas_call(
        functools.partial(_combine_kernel, tc=tc, np_tiles=np_tiles),
        grid=(nc,),
        in_specs=[idx, idx, pl.BlockSpec((tc, d), lambda c: (c, 0)),
                  pl.BlockSpec((tc, ROUTER_LANES), lambda c: (c, 0)),
                  pl.BlockSpec((1, d), lambda c: (0, 0)), ANY_SPEC],
        out_specs=[pl.BlockSpec((tc, d), lambda c: (jnp.minimum(c, np_tiles - 1), 0)),
                   pl.BlockSpec((tc, d), lambda c: (jnp.maximum(c - np_tiles, 0), 0))],
        out_shape=[jax.ShapeDtypeStruct((n_p, d), F32), jax.ShapeDtypeStruct((n - n_p, d), F32)],
        scratch_shapes=[pltpu.VMEM((tc, d), F32), pltpu.VMEM((tc, d), F32),
                        pltpu.SemaphoreType.DMA],
        compiler_params=_cparams("arbitrary"),
        name="moe_combine_norm",
    )(d1.reshape(nc, 1, tc), d2.reshape(nc, 1, tc), x, route, g.reshape(1, d), yg)


def moe_plan(route):
    n = route.shape[0]
    sub = MOE_TILE // MOE_SUBS
    i1 = route[:, 0].astype(jnp.int32)
    i2 = route[:, 1].astype(jnp.int32)
    experts = jnp.arange(N_EXPERTS, dtype=jnp.int32)
    onehot = ((i1[:, None] == experts) | (i2[:, None] == experts)).astype(jnp.int32)
    counts = jnp.sum(onehot, axis=0)
    subs = (counts + sub - 1) // sub
    tiles = (counts + MOE_TILE - 1) // MOE_TILE
    tile_end = jnp.cumsum(tiles)
    tile_start = tile_end - tiles
    pos = jnp.cumsum(onehot, axis=0) - onehot
    dest = (tile_start * MOE_TILE)[None, :] + pos
    d1 = jnp.sum(jnp.where(i1[:, None] == experts, dest, 0), axis=1)
    d2 = jnp.sum(jnp.where(i2[:, None] == experts, dest, 0), axis=1)
    n_tiles = -(-(2 * n + N_EXPERTS * (MOE_TILE - 1)) // MOE_TILE)
    r = jnp.arange(n_tiles, dtype=jnp.int32)
    owner = jnp.minimum(jnp.sum((r[:, None] >= tile_end[None, :]).astype(jnp.int32), axis=1),
                        N_EXPERTS - 1)
    is_owner = owner[:, None] == experts[None, :]
    first_tile = jnp.sum(jnp.where(is_owner, tile_start[None, :], 0), axis=1)
    owner_subs = jnp.sum(jnp.where(is_owner, subs[None, :], 0), axis=1)
    n_used = tile_end[-1]
    n_sub = jnp.where(r < n_used, jnp.clip(owner_subs - (r - first_tile) * MOE_SUBS, 0, MOE_SUBS), 0)
    last_expert = jnp.sum(jnp.where(r == n_used - 1, owner, 0))
    tile_expert = jnp.where(r < n_used, owner, last_expert).astype(jnp.int32)
    tokens = jnp.arange(n, dtype=jnp.int32)
    row_token = jnp.zeros((n_tiles * MOE_TILE,), jnp.int32).at[jnp.concatenate([d1, d2])].set(
        jnp.concatenate([tokens, tokens]), unique_indices=True)
    return d1, d2, row_token, tile_expert, n_sub.astype(jnp.int32)


def _hgrn_kernel(p_ref, lb_ref, hn_ref, s0_ref, out_hbm, oa_ref, sout_ref,
                 st_ref, q_s, k_s, v_s, b_s, o_s, *, tt):
    del out_hbm
    c_len = HGRN_CHUNK
    t = pl.program_id(1)

    @pl.when(t == 0)
    def _():
        for h in range(N_HEADS):
            st_ref[h] = s0_ref[0, h].T

    lb = lb_ref[...]
    keep = 1.0 - lb
    z = p_ref[:, D_BRANCH:2 * D_BRANCH]
    e = jnp.exp(-jnp.abs(z))
    r = 1.0 / (1.0 + e)
    er = e * r
    logf = jnp.log(jnp.maximum(lb, LB_FLOOR) + keep * jnp.where(z >= 0.0, r, er))
    k_s[...] = keep * jnp.where(z >= 0.0, er, r)
    q_s[...] = _silu(p_ref[:, 0:D_BRANCH])
    v_s[...] = p_ref[:, 2 * D_BRANCH:3 * D_BRANCH]
    row = lax.broadcasted_iota(jnp.int32, (tt, tt), 0)
    col = lax.broadcasted_iota(jnp.int32, (tt, tt), 1)
    tri = jnp.where(((row ^ col) < c_len) & (col <= row), 1.0, 0.0).astype(BF16)
    lf_hi, lf_lo = _split_bf16(logf * LOG2_E)
    b_s[...] = _dot(tri, lf_hi) + _dot(tri, lf_lo)

    ones = jnp.ones((D_HEAD, D_HEAD), BF16)
    row_c = lax.broadcasted_iota(jnp.int32, (c_len, 1), 0)

    def chunk(c, carry):
        r0 = pl.multiple_of(c * c_len, c_len)
        qc = q_s[pl.ds(r0, c_len), :]
        kc = k_s[pl.ds(r0, c_len), :]
        vc = v_s[pl.ds(r0, c_len), :]
        bc = b_s[pl.ds(r0, c_len), :]
        b_last = bc[c_len - 1:c_len, :]
        qt = (qc * jnp.exp2(bc)).astype(BF16)
        kt = (kc * jnp.exp2(b_last - bc)).astype(BF16)
        decay = jnp.exp2(b_last)
        vb = vc.astype(BF16)
        xs = []
        for s in range(c_len):
            diff = jnp.where(row_c >= s, bc - bc[s:s + 1, :], NEG_BIG)
            xs.append((qc * kc[s:s + 1, :] * jnp.exp2(diff)).astype(BF16))
        x_all = jnp.concatenate(xs, axis=0)
        heads = [slice(h * D_HEAD, (h + 1) * D_HEAD) for h in range(N_HEADS)]
        states = [st_ref[h] for h in range(N_HEADS)]
        carried = [_dot_nt(qt[:, hs], st.astype(BF16)) for hs, st in zip(heads, states)]
        updates = [_dot_tn(vb[:, hs], kt[:, hs]) for hs in heads]
        sums = [_dot(x_all[:, hs], ones) for hs in heads]
        outs = []
        for hs, acc, y in zip(heads, carried, sums):
            for s in range(c_len):
                acc = acc + y[s * c_len:(s + 1) * c_len, :] * vc[s:s + 1, hs]
            outs.append(acc)
        for h, (hs, st, up) in enumerate(zip(heads, states, updates)):
            st_ref[h] = st * decay[:, hs] + up
        o_s[pl.ds(r0, c_len), :] = jnp.concatenate(outs, axis=1)
        return carry

    lax.fori_loop(0, tt // c_len, chunk, 0, unroll=HGRN_UNROLL)

    gate = _silu(p_ref[:, 3 * D_BRANCH:4 * D_BRANCH])
    hn = hn_ref[...]
    for h in range(N_HEADS):
        hs = slice(h * D_HEAD, (h + 1) * D_HEAD)
        oh = o_s[:, hs]
        oh = oh * lax.rsqrt(jnp.mean(oh * oh, axis=-1, keepdims=True) + EPS) * hn
        oa_ref[:, hs] = (oh * gate[:, hs]).astype(oa_ref.dtype)

    @pl.when(t == pl.num_programs(1) - 1)
    def _():
        for h in range(N_HEADS):
            sout_ref[0, h] = st_ref[h].T


def branch_buffer(n_rows):
    return jnp.zeros((n_rows, D_BRANCH), BF16)


def hgrn_mixer(proj, lb, hnorm, s0, row0, n_seq, seq_len, tt, out):
    nt = seq_len // tt
    rb0 = row0 // tt
    scr = [pltpu.VMEM((N_HEADS, D_HEAD, D_HEAD), F32)] + [pltpu.VMEM((tt, D_BRANCH), F32)] * 5
    return pl.pallas_call(
        functools.partial(_hgrn_kernel, tt=tt),
        grid=(n_seq, nt),
        in_specs=[pl.BlockSpec((tt, 4 * D_BRANCH), lambda s, t: (rb0 + s * nt + t, 0)),
                  pl.BlockSpec((1, D_BRANCH), lambda s, t: (0, 0)),
                  pl.BlockSpec((1, D_HEAD), lambda s, t: (0, 0)),
                  pl.BlockSpec((1, N_HEADS, D_HEAD, D_HEAD), lambda s, t: (s, 0, 0, 0)), ANY_SPEC],
        out_specs=[pl.BlockSpec((tt, D_BRANCH), lambda s, t: (rb0 + s * nt + t, 0)),
                   pl.BlockSpec((1, N_HEADS, D_HEAD, D_HEAD), lambda s, t: (s, 0, 0, 0))],
        out_shape=[jax.ShapeDtypeStruct(out.shape, out.dtype),
                   jax.ShapeDtypeStruct((n_seq, N_HEADS, D_HEAD, D_HEAD), F32)],
        scratch_shapes=scr,
        input_output_aliases={4: 0},
        compiler_params=_cparams("arbitrary", "arbitrary"),
        name="hgrn_mixer",
    )(proj, lb.reshape(1, D_BRANCH), hnorm.reshape(1, D_HEAD), s0, out)


def _pool_conv_kernel(p_ref, hp_ref, hc_ref, wp_ref, ps_ref, cw_ref, ob_hbm, oc_hbm,
                      ob_ref, oc_ref, pn_ref, cn_ref, xe, ue, *, tt, pos0):
    del ob_hbm, oc_hbm
    t = pl.program_id(1)

    @pl.when(t == 0)
    def _():
        xe[0:POOL_HALO, :] = hp_ref[0]
        ue[0:CONV_HALO, :] = hc_ref[0]

    @pl.when(t > 0)
    def _():
        xe[0:POOL_HALO, :] = xe[tt:tt + POOL_HALO, :]
        ue[0:CONV_HALO, :] = ue[tt:tt + CONV_HALO, :]

    x = p_ref[:, 0:D_BRANCH]
    xe[POOL_HALO:POOL_HALO + tt, :] = x
    u = p_ref[:, 3 * D_BRANCH:4 * D_BRANCH] * p_ref[:, D_BRANCH:2 * D_BRANCH]
    ue[CONV_HALO:CONV_HALO + tt, :] = u

    pos = pos0 + t * tt + lax.broadcasted_iota(jnp.int32, (tt, 1), 0)
    group = D_BRANCH // len(POOL_WINDOWS)
    for g, w in enumerate(POOL_WINDOWS):
        ls = slice(g * group, (g + 1) * group)
        s = x[:, ls]
        for j in range(1, w):
            s = s + xe[POOL_HALO - j:POOL_HALO - j + tt, ls]
        cnt = jnp.minimum(pos + 1, w).astype(F32)
        dlt = s / cnt - x[:, ls]
        y = _dot(dlt.astype(BF16), wp_ref[g].astype(BF16))
        ob_ref[:, ls] = (y * ps_ref[:, ls]).astype(ob_ref.dtype)

    y = ue[CONV_HALO - 2:CONV_HALO - 2 + tt, :] * cw_ref[0:1, :]
    for j in range(1, CONV_W):
        y = y + ue[CONV_HALO - 2 + j:CONV_HALO - 2 + j + tt, :] * cw_ref[j:j + 1, :]
    oc_ref[...] = (p_ref[:, 2 * D_BRANCH:3 * D_BRANCH] * y).astype(oc_ref.dtype)

    pn_ref[0] = xe[tt:tt + POOL_HALO, :]
    cn_ref[0] = ue[tt:tt + CONV_HALO, :]


def pool_conv_mixer(proj, hist_pool, hist_conv, w_pool, pool_scale, conv_w,
                    row0, n_seq, seq_len, tt, pos0, outs):
    nt = seq_len // tt
    rb0 = row0 // tt
    out_shape = jax.ShapeDtypeStruct(outs[0].shape, outs[0].dtype)
    cw = jnp.zeros((8, D_BRANCH), F32).at[:CONV_W].set(conv_w)
    out_row = pl.BlockSpec((tt, D_BRANCH), lambda s, t: (rb0 + s * nt + t, 0))
    return pl.pallas_call(
        functools.partial(_pool_conv_kernel, tt=tt, pos0=pos0),
        grid=(n_seq, nt),
        in_specs=[pl.BlockSpec((tt, 4 * D_BRANCH), lambda s, t: (rb0 + s * nt + t, 1)),
                  pl.BlockSpec((1, POOL_HALO, D_BRANCH), lambda s, t: (s, 0, 0)),
                  pl.BlockSpec((1, CONV_HALO, D_BRANCH), lambda s, t: (s, 0, 0)),
                  pl.BlockSpec(w_pool.shape, lambda s, t: (0, 0, 0)),
                  pl.BlockSpec((1, D_BRANCH), lambda s, t: (0, 0)),
                  pl.BlockSpec((8, D_BRANCH), lambda s, t: (0, 0)), ANY_SPEC, ANY_SPEC],
        out_specs=[out_row, out_row,
                   pl.BlockSpec((1, POOL_HALO, D_BRANCH), lambda s, t: (s, 0, 0)),
                   pl.BlockSpec((1, CONV_HALO, D_BRANCH), lambda s, t: (s, 0, 0))],
        out_shape=[out_shape, out_shape,
                   jax.ShapeDtypeStruct((n_seq, POOL_HALO, D_BRANCH), F32),
                   jax.ShapeDtypeStruct((n_seq, CONV_HALO, D_BRANCH), F32)],
        scratch_shapes=[pltpu.VMEM((tt + POOL_HALO, D_BRANCH), F32),
                        pltpu.VMEM((tt + CONV_HALO, D_BRANCH), F32)],
        input_output_aliases={6: 0, 7: 1},
        compiler_params=_cparams("arbitrary", "arbitrary"),
        name="pool_conv_mixer",
    )(proj, hist_pool, hist_conv, w_pool, pool_scale.reshape(1, D_BRANCH), cw, *outs)


def _sb_tile(q_ref, kv, o_acc, run, masked):
    tq = q_ref.shape[0]
    scale = D_HEAD ** -0.5
    kr = lax.broadcasted_iota(jnp.int32, (SB_TILE, 2 * SB_TILE), 0)
    kc = lax.broadcasted_iota(jnp.int32, (SB_TILE, 2 * SB_TILE), 1)
    sums = jnp.where((kc >= SB_TILE) | (kr > kc), 1.0, 0.0).astype(BF16)
    if masked:
        qi = lax.broadcasted_iota(jnp.int32, (tq, SB_TILE), 0)
        ki = lax.broadcasted_iota(jnp.int32, (tq, SB_TILE), 1)
        mask = ki < qi
    run_old = run[...]
    heads = [slice(h * D_HEAD, (h + 1) * D_HEAD) for h in range(N_HEADS)]
    kvs = [kv(h) for h in range(N_HEADS)]
    zs = [_dot_nt(q_ref[:, hs], k.astype(BF16)) * scale for hs, (k, _) in zip(heads, kvs)]
    lks = []
    for z in zs:
        log_keep = -(jnp.maximum(z, 0.0) + jnp.log1p(jnp.exp(-jnp.abs(z))))
        lks.append(jnp.where(mask, log_keep, 0.0) if masked else log_keep)
    splits = [_split_bf16(lk) for lk in lks]
    sms = [_dot(hi, sums) + _dot(lo, sums) for hi, lo in splits]
    probs = []
    for hs, z, lk, sm in zip(heads, zs, lks, sms):
        a = jnp.exp(z + lk + sm[:, 0:SB_TILE] + run_old[:, hs])
        probs.append((jnp.where(mask, a, 0.0) if masked else a).astype(BF16))
    o_new = [_dot(a, v.astype(BF16)) for a, (_, v) in zip(probs, kvs)]
    o_acc[...] += jnp.concatenate(o_new, axis=1)
    run[...] = run_old + jnp.concatenate([sm[:, SB_TILE:2 * SB_TILE] for sm in sms], axis=1)


def _sb_alive(run):
    return (jnp.max(run[...]) > SB_DONE).astype(jnp.int32)


def _sb_past_loop(q_ref, kv_tile, o_acc, run, n_tiles):
    def cond(c):
        j, alive = c
        return (j >= 0) & (alive > 0)

    def body(c):
        j, _ = c
        r0 = pl.multiple_of(j * SB_TILE, SB_TILE)
        _sb_tile(q_ref, functools.partial(kv_tile, r0), o_acc, run, masked=False)
        return j - 1, _sb_alive(run)

    lax.while_loop(cond, body, (jnp.int32(n_tiles) - 1, _sb_alive(run)))


def _sb_prompt_kernel(q_ref, k_ref, v_ref, out_hbm, o_ref, o_acc, run):
    del out_hbm
    i = pl.program_id(1)
    o_acc[...] = jnp.zeros_like(o_acc)
    run[...] = jnp.zeros_like(run)

    def kv_tile(r0, h):
        hs = slice(h * D_HEAD, (h + 1) * D_HEAD)
        return k_ref[pl.ds(r0, SB_TILE), hs], v_ref[pl.ds(r0, SB_TILE), hs]

    _sb_tile(q_ref, functools.partial(kv_tile, pl.multiple_of(i * SB_TILE, SB_TILE)), o_acc, run,
             masked=True)
    _sb_past_loop(q_ref, kv_tile, o_acc, run, i)
    o_ref[...] = o_acc[...].astype(o_ref.dtype)


def sb_prompt_mixer(qkv, n_seq, seq_len, out):
    nq = seq_len // SB_TILE
    seq = lambda c: pl.BlockSpec((seq_len, D_BRANCH), lambda s, i: (s, c))
    return pl.pallas_call(
        _sb_prompt_kernel,
        grid=(n_seq, nq),
        in_specs=[pl.BlockSpec((SB_TILE, D_BRANCH), lambda s, i: (s * nq + i, 0)), seq(1), seq(2),
                  ANY_SPEC],
        out_specs=pl.BlockSpec((SB_TILE, D_BRANCH), lambda s, i: (s * nq + i, 0)),
        out_shape=jax.ShapeDtypeStruct(out.shape, out.dtype),
        scratch_shapes=[pltpu.VMEM((SB_TILE, D_BRANCH), F32), pltpu.VMEM((SB_TILE, D_BRANCH), F32)],
        input_output_aliases={3: 0},
        compiler_params=_cparams("arbitrary", "arbitrary"),
        name="sb_prompt_mixer",
    )(qkv, qkv, qkv, out)


def _sb_sample_kernel(qkv_ref, kp_ref, vp_ref, out_hbm, o_ref, o_acc, run, *, n_past_tiles):
    del out_hbm
    o_acc[...] = jnp.zeros_like(o_acc)
    run[...] = jnp.zeros_like(run)
    tq = qkv_ref.shape[0]
    q_ref = qkv_ref.at[:, 0:D_BRANCH]
    pad = jnp.zeros((SB_TILE - tq, D_BRANCH), qkv_ref.dtype)
    k_new = jnp.concatenate([qkv_ref[:, D_BRANCH:2 * D_BRANCH], pad], axis=0)
    v_new = jnp.concatenate([qkv_ref[:, 2 * D_BRANCH:3 * D_BRANCH], pad], axis=0)

    def kv_new(h):
        hs = slice(h * D_HEAD, (h + 1) * D_HEAD)
        return k_new[:, hs], v_new[:, hs]

    def kv_past(r0, h):
        return kp_ref[pl.ds(r0, SB_TILE), h, :], vp_ref[pl.ds(r0, SB_TILE), h, :]

    _sb_tile(q_ref, kv_new, o_acc, run, masked=True)
    _sb_past_loop(q_ref, kv_past, o_acc, run, n_past_tiles)
    o_ref[...] = o_acc[...].astype(o_ref.dtype)


def sb_sample_mixer(qkv, k_cache, v_cache, layer, row0, n_seq, seq_len, out):
    past = k_cache.shape[2]
    rb0 = row0 // seq_len
    cache = pl.BlockSpec((None, None, past, N_HEADS, D_HEAD), lambda s: (layer, s, 0, 0, 0))
    return pl.pallas_call(
        functools.partial(_sb_sample_kernel, n_past_tiles=past // SB_TILE),
        grid=(n_seq,),
        in_specs=[pl.BlockSpec((seq_len, 3 * D_BRANCH), lambda s: (rb0 + s, 0)), cache, cache, ANY_SPEC],
        out_specs=pl.BlockSpec((seq_len, D_BRANCH), lambda s: (rb0 + s, 0)),
        out_shape=jax.ShapeDtypeStruct(out.shape, out.dtype),
        scratch_shapes=[pltpu.VMEM((seq_len, D_BRANCH), F32), pltpu.VMEM((seq_len, D_BRANCH), F32)],
        input_output_aliases={3: 0},
        compiler_params=_cparams("arbitrary"),
        name="sb_sample_mixer",
    )(qkv, k_cache, v_cache, out)


def _tile(n, want):
    if n <= want:
        return n
    for t in range(want, 7, -8):
        if n % t == 0:
            return t
    return n


def kernel(x_prompt, x_sample, state_hgrn, state_pool, state_conv, cache_k, cache_v, norm_mix, norm_ffn, norm_final, w_in, hgrn_lower_bound, hgrn_out_norm, pool_w, pool_scale, conv_w, w_branch, w_gate, b_gate, w_out, ffn_w1, ffn_w3, ffn_w2, moe_router, moe_w1, moe_w3, moe_w2):
    depth = w_in.shape[0]
    pb, pt, d = x_prompt.shape
    sb, st, _ = x_sample.shape
    past = cache_k.shape[2]
    n_p = pb * pt
    n_s = sb * st
    n = n_p + n_s
    assert depth % 2 == 0, "the last layer is expected to be a MoE layer"
    x = (x_prompt.reshape(n_p, d), x_sample.reshape(n_s, d))

    tm_row = _tile(n_s, 512)
    tm_mm = _tile(n_s, 2048)
    tm_half = tm_mm // 2
    tc = _tile(n_s, 256)
    tt_p = _tile(pt, 256)
    assert n_p % tm_mm == 0 and n_p % tc == 0 and n_p % tm_row == 0

    sm = jax.nn.softmax(hgrn_lower_bound.astype(F32), axis=0)
    lbs = jnp.cumsum(sm, axis=0) - sm[0:1]

    zeros_state = jnp.zeros((pb, N_HEADS, D_HEAD, D_HEAD), F32)
    zeros_pool = jnp.zeros((pb, POOL_HALO, D_BRANCH), F32)
    zeros_conv = jnp.zeros((pb, CONV_HALO, D_BRANCH), F32)
    pool_pad = ((0, 0), (0, 0), (POOL_HALO - state_pool.shape[2], 0), (0, 0))
    conv_pad = ((0, 0), (0, 0), (CONV_HALO - state_conv.shape[2], 0), (0, 0))
    sample_pool = jnp.pad(state_pool, pool_pad)
    sample_conv = jnp.pad(state_conv, conv_pad)

    outs = {k: [] for k in ("p_h", "p_pool", "p_conv", "s_h", "s_pool", "s_conv")}
    caches = cache_buffers(depth, n_p, n_s)
    xn = rmsnorm(x, norm_mix[0], BF16, tm_row)
    y_p = y_s = None
    for l in range(depth):
        proj = in_proj(xn, w_in, l, 8 * D_BRANCH, tm_mm, D_BRANCH)
        qkv16, *caches = qkv_proj(xn, w_in, l, 8 * D_BRANCH, n_p, caches, tm_half)

        oa, hs_p = hgrn_mixer(proj, lbs[l], hgrn_out_norm[l], zeros_state, 0, pb, pt, tt_p,
                              branch_buffer(n))
        oa, hs_s = hgrn_mixer(proj, lbs[l], hgrn_out_norm[l], state_hgrn[l], n_p, sb, st, st, oa)
        ob, oc, pool_p, conv_p = pool_conv_mixer(
            proj, zeros_pool, zeros_conv, pool_w[l], pool_scale[l], conv_w[l], 0, pb, pt, tt_p, 0,
            (branch_buffer(n), branch_buffer(n)))
        ob, oc, pool_s, conv_s = pool_conv_mixer(
            proj, sample_pool[l], sample_conv[l], pool_w[l], pool_scale[l], conv_w[l],
            n_p, sb, st, st, past, (ob, oc))
        od = sb_prompt_mixer(qkv16, pb, pt, branch_buffer(n))
        od = sb_sample_mixer(qkv16, cache_k, cache_v, l, n_p, sb, st, od)

        merged = gate_merge(xn, (oa, ob, oc, od), w_gate, b_gate, w_branch, l, tm_half, 256)
        x = out_proj(merged, w_out, l, x, tm_mm, D_BRANCH)

        outs["p_h"].append(hs_p)
        outs["s_h"].append(hs_s)
        outs["p_pool"].append(pool_p[:, 1:])
        outs["s_pool"].append(pool_s[:, 1:])
        outs["p_conv"].append(conv_p[:, CONV_HALO - (CONV_W - 1):])
        outs["s_conv"].append(conv_s[:, CONV_HALO - (CONV_W - 1):])

        if l % 2 == 0:
            h = rmsnorm(x, norm_ffn[l], BF16, tm_row)
            act = dense_glu(h, ffn_w1, ffn_w3, l // 2, tm_mm, D_BRANCH)
            x = out_proj(act, ffn_w2, l // 2, x, tm_half, 256)
            xn = rmsnorm(x, norm_mix[l + 1], BF16, tm_row)
        else:
            h32, route = rmsnorm_router(x, norm_ffn[l], moe_router[l // 2], tm_row)
            d1, d2, row_token, tile_expert, n_sub = moe_plan(route)
            act = grouped_glu(h32, row_token, tile_expert, n_sub, moe_w1, moe_w3, l // 2,
                              D_BRANCH)
            yg = grouped_down(act, tile_expert, n_sub, moe_w2, l // 2, MOE_TILE // 2, 256)
            if l + 1 < depth:
                raise NotImplementedError("a MoE layer followed by another layer")
            y_p, y_s = moe_combine_norm(x, route, yg, d1, d2, norm_final, n_p, tc)

    pk, pv, sk, sv = caches
    st_ = lambda k: jnp.stack(outs[k])
    return (y_p.reshape(pb, pt, d), y_s.reshape(sb, st, d),
            st_("p_h"), st_("p_pool"), st_("p_conv"),
            pk.reshape(depth, pb, pt, N_HEADS, D_HEAD), pv.reshape(depth, pb, pt, N_HEADS, D_HEAD),
            st_("s_h"), st_("s_pool"), st_("s_conv"),
            sk.reshape(depth, sb, st, N_HEADS, D_HEAD), sv.reshape(depth, sb, st, N_HEADS, D_HEAD))
```

```python
import functools

import jax
import jax.numpy as jnp
from jax import lax
from jax.experimental import pallas as pl
from jax.experimental.pallas import tpu as pltpu

F32 = jnp.float32
BF16 = jnp.bfloat16

EPS = 1e-6
NEG_BIG = -1e30
LB_FLOOR = 1e-30
LOG2_E = 1.4426950408889634
N_HEADS = 4
D_HEAD = 128
D_BRANCH = N_HEADS * D_HEAD
POOL_WINDOWS = (2, 4, 8, 16)
POOL_HALO = 16
CONV_W = 3
CONV_HALO = 8
N_EXPERTS = 8
HGRN_CHUNK = 16
HGRN_UNROLL = 4
SB_TILE = 128
SB_DONE = -120.0
ROUTER_LANES = 128
MOE_TILE = 2048
MOE_SUBS = 4
DMA_UNROLL = 8

V7X_VMEM_LIMIT_BYTES = 56 * 1024 * 1024

ANY_SPEC = pl.BlockSpec(memory_space=pl.ANY)


def _cparams(*sem):
    return pltpu.CompilerParams(dimension_semantics=sem,
                                vmem_limit_bytes=V7X_VMEM_LIMIT_BYTES)


def _split_bf16(x):
    hi = x.astype(BF16)
    lo = (x - hi.astype(F32)).astype(BF16)
    return hi, lo


def _dot(a, b):
    return jnp.dot(a, b, preferred_element_type=F32)


def _dot_nt(a, b):
    return lax.dot_general(a, b, (((1,), (1,)), ((), ())), preferred_element_type=F32)


def _dot_tn(a, b):
    return lax.dot_general(a, b, (((0,), (0,)), ((), ())), preferred_element_type=F32)


def _silu(a):
    return a * jax.nn.sigmoid(a)


def _rmsnorm_rows(x, g):
    return x * lax.rsqrt(jnp.mean(x * x, axis=-1, keepdims=True) + EPS) * g


def _two_source_specs(xs, tm, width, col_of):
    def index_map(i, *j, t0, nt):
        inside = (i >= t0) & (i < t0 + nt)
        return jnp.clip(i - t0, 0, nt - 1), jnp.where(inside, col_of(*j), 0)

    starts, specs, t0 = [], [], 0
    for x in xs:
        nt = x.shape[0] // tm
        starts.append(t0)
        specs.append(pl.BlockSpec((tm, width), functools.partial(index_map, t0=t0, nt=nt)))
        t0 += nt
    return specs, starts, t0


def _pick_source(refs, starts):
    i = pl.program_id(0)
    x = refs[0][...]
    for ref, t0 in zip(refs[1:], starts[1:]):
        x = jnp.where(i >= t0, ref[...], x)
    return x


def _rms_kernel(*refs, starts):
    g_ref, o_ref = refs[-2:]
    x = _pick_source(refs[:-2], starts)
    o_ref[...] = _rmsnorm_rows(x, g_ref[...]).astype(o_ref.dtype)


def rmsnorm(xs, g, out_dtype, tm):
    xs = xs if isinstance(xs, tuple) else (xs,)
    d = xs[0].shape[1]
    specs, starts, n_tiles = _two_source_specs(xs, tm, d, lambda: 0)
    return pl.pallas_call(
        functools.partial(_rms_kernel, starts=starts),
        grid=(n_tiles,),
        in_specs=specs + [pl.BlockSpec((1, d), lambda i: (0, 0))],
        out_specs=pl.BlockSpec((tm, d), lambda i: (i, 0)),
        out_shape=jax.ShapeDtypeStruct((n_tiles * tm, d), out_dtype),
        compiler_params=_cparams("arbitrary"),
        name="rmsnorm",
    )(*xs, g.reshape(1, d))


def _route(h, r_ref, h32_ref, route_ref):
    h_hi, h_lo = _split_bf16(h)
    h32_ref[...] = h_hi.astype(F32)
    r_hi, r_lo = _split_bf16(r_ref[...])
    logits = _dot(h_hi, r_hi) + (_dot(h_lo, r_hi) + _dot(h_hi, r_lo))
    lane = lax.broadcasted_iota(jnp.int32, logits.shape, 1).astype(F32)
    neg_inf = jnp.float32(-jnp.inf)
    l1 = jnp.where(lane < N_EXPERTS, logits, neg_inf)
    m1 = jnp.max(l1, axis=-1, keepdims=True)
    i1 = jnp.min(jnp.where(l1 == m1, lane, float(ROUTER_LANES)), axis=-1, keepdims=True)
    l2 = jnp.where(lane == i1, neg_inf, l1)
    m2 = jnp.max(l2, axis=-1, keepdims=True)
    i2 = jnp.min(jnp.where(l2 == m2, lane, float(ROUTER_LANES)), axis=-1, keepdims=True)
    e = jnp.exp(m2 - m1)
    w1 = 1.0 / (1.0 + e)
    w2 = e / (1.0 + e)
    route_ref[...] = jnp.where(lane == 0.0, i1, jnp.where(lane == 1.0, i2, jnp.where(
        lane == 2.0, w1, jnp.where(lane == 3.0, w2, 0.0))))


def _rms_router_kernel(x_ref, g_ref, r_ref, h32_ref, route_ref):
    _route(_rmsnorm_rows(x_ref[...], g_ref[...]), r_ref, h32_ref, route_ref)


def rmsnorm_router(x, g, router, tm):
    n, d = x.shape
    r_pad = jnp.zeros((d, ROUTER_LANES), F32).at[:, :N_EXPERTS].set(router)
    row = pl.BlockSpec((tm, d), lambda i: (i, 0))
    return pl.pallas_call(
        _rms_router_kernel,
        grid=(n // tm,),
        in_specs=[row, pl.BlockSpec((1, d), lambda i: (0, 0)),
                  pl.BlockSpec((d, ROUTER_LANES), lambda i: (0, 0))],
        out_specs=[row, pl.BlockSpec((tm, ROUTER_LANES), lambda i: (i, 0))],
        out_shape=[jax.ShapeDtypeStruct((n, d), F32), jax.ShapeDtypeStruct((n, ROUTER_LANES), F32)],
        compiler_params=_cparams("parallel"),
        name="rmsnorm_router",
    )(x, g.reshape(1, d), r_pad)


def _mix_out_kernel(m_ref, w_ref, *refs, starts):
    g_ref, xo_ref, h_ref = refs[-3:]
    x = _pick_source(refs[:-3], starts) + _dot(m_ref[...], w_ref[...])
    xo_ref[...] = x
    h_ref[...] = _rmsnorm_rows(x, g_ref[...]).astype(h_ref.dtype)


def mix_out(merged, w, xs, g, tm):
    xs = xs if isinstance(xs, tuple) else (xs,)
    n, k = merged.shape
    d = w.shape[1]
    specs, starts, n_tiles = _two_source_specs(xs, tm, d, lambda: 0)
    assert n_tiles * tm == n
    row = pl.BlockSpec((tm, d), lambda i: (i, 0))
    return pl.pallas_call(
        functools.partial(_mix_out_kernel, starts=starts),
        grid=(n_tiles,),
        in_specs=[pl.BlockSpec((tm, k), lambda i: (i, 0)), pl.BlockSpec((k, d), lambda i: (0, 0))]
                 + specs + [pl.BlockSpec((1, d), lambda i: (0, 0))],
        out_specs=[row, row],
        out_shape=[jax.ShapeDtypeStruct((n, d), F32), jax.ShapeDtypeStruct((n, d), BF16)],
        compiler_params=_cparams("arbitrary"),
        name="mix_out",
    )(merged, w, *xs, g.reshape(1, d))


def _proj_kernel(x_ref, w_ref, o_ref):
    o_ref[...] = _dot(x_ref[...], w_ref[...].astype(BF16)).astype(o_ref.dtype)


def in_proj(xn, w_all, layer, n_cols, tm, tn):
    n, k = xn.shape
    return pl.pallas_call(
        _proj_kernel,
        grid=(n // tm, n_cols // tn),
        in_specs=[pl.BlockSpec((tm, k), lambda i, j: (i, 0)),
                  pl.BlockSpec((None, k, tn), lambda i, j: (layer, 0, j))],
        out_specs=pl.BlockSpec((tm, tn), lambda i, j: (i, j)),
        out_shape=jax.ShapeDtypeStruct((n, n_cols), F32),
        compiler_params=_cparams("parallel", "arbitrary"),
        name="in_proj",
    )(xn, w_all)


def _qkv_kernel(x_ref, w_ref, pk_hbm, pv_hbm, sk_hbm, sv_hbm,
                o16_ref, pk_ref, pv_ref, sk_ref, sv_ref, *, np_tiles):
    del pk_hbm, pv_hbm, sk_hbm, sv_hbm
    r = _dot(x_ref[...], w_ref[...].astype(BF16))
    o16_ref[...] = r.astype(o16_ref.dtype)
    is_prompt = pl.program_id(0) < np_tiles
    j = pl.program_id(1)

    def put(dst):
        for h in range(N_HEADS):
            dst[:, h, :] = r[:, h * D_HEAD:(h + 1) * D_HEAD]

    for col, p_ref, s_ref in ((1, pk_ref, sk_ref), (2, pv_ref, sv_ref)):
        pl.when((j == col) & is_prompt)(functools.partial(put, p_ref))
        pl.when((j == col) & jnp.logical_not(is_prompt))(functools.partial(put, s_ref))


def cache_buffers(depth, n_p, n_s):
    return [jnp.zeros((depth * n, N_HEADS, D_HEAD), F32) for n in (n_p, n_p, n_s, n_s)]


def qkv_proj(xn, w_all, layer, col_start, n_p, caches, tm):
    n, k = xn.shape
    n_s = n - n_p
    np_tiles, ns_tiles = n_p // tm, n_s // tm
    cb0 = col_start // D_BRANCH
    head_blk = (tm, N_HEADS, D_HEAD)
    p_spec = pl.BlockSpec(head_blk, lambda i, j: (layer * np_tiles + jnp.minimum(i, np_tiles - 1), 0, 0))
    s_spec = pl.BlockSpec(head_blk, lambda i, j: (layer * ns_tiles + jnp.maximum(i - np_tiles, 0), 0, 0))
    return pl.pallas_call(
        functools.partial(_qkv_kernel, np_tiles=np_tiles),
        grid=(n // tm, 3),
        in_specs=[pl.BlockSpec((tm, k), lambda i, j: (i, 0)),
                  pl.BlockSpec((None, k, D_BRANCH), lambda i, j: (layer, 0, cb0 + j))]
                 + [ANY_SPEC] * 4,
        out_specs=[pl.BlockSpec((tm, D_BRANCH), lambda i, j: (i, j)), p_spec, p_spec, s_spec, s_spec],
        out_shape=[jax.ShapeDtypeStruct((n, 3 * D_BRANCH), BF16)]
                  + [jax.ShapeDtypeStruct(c.shape, c.dtype) for c in caches],
        input_output_aliases={2: 1, 3: 2, 4: 3, 5: 4},
        compiler_params=_cparams("arbitrary", "arbitrary"),
        name="qkv_proj",
    )(xn, w_all, *caches)


def _out_proj_kernel(m_ref, w_ref, *refs, starts):
    o_ref = refs[-1]
    o_ref[...] = _pick_source(refs[:-1], starts) + _dot(m_ref[...], w_ref[...].astype(BF16))


def out_proj(a, w_all, layer, xs, tm, tn):
    xs = xs if isinstance(xs, tuple) else (xs,)
    n, k = a.shape
    d = xs[0].shape[1]
    specs, starts, n_tiles = _two_source_specs(xs, tm, tn, lambda j: j)
    assert n_tiles * tm == n
    return pl.pallas_call(
        functools.partial(_out_proj_kernel, starts=starts),
        grid=(n_tiles, d // tn),
        in_specs=[pl.BlockSpec((tm, k), lambda i, j: (i, 0)),
                  pl.BlockSpec((None, k, tn), lambda i, j: (layer, 0, j))] + specs,
        out_specs=pl.BlockSpec((tm, tn), lambda i, j: (i, j)),
        out_shape=jax.ShapeDtypeStruct((n, d), F32),
        compiler_params=_cparams("arbitrary", "arbitrary"),
        name="out_proj",
    )(a, w_all, *xs)


def _gate_merge_kernel(xn_ref, oa_ref, ob_ref, oc_ref, od_ref,
                       wg0_ref, wg1_ref, wg2_ref, wg3_ref,
                       bg0_ref, bg1_ref, bg2_ref, bg3_ref, wb_ref, o_ref):
    xn = xn_ref[...]
    branches = (oa_ref, ob_ref, oc_ref, od_ref)
    gates = (wg0_ref, wg1_ref, wg2_ref, wg3_ref)
    biases = (bg0_ref, bg1_ref, bg2_ref, bg3_ref)
    acc = None
    for n in range(4):
        g = jax.nn.sigmoid(_dot(xn, gates[n][...].astype(BF16)) + biases[n][...])
        p = _dot(branches[n][...], wb_ref[n].astype(BF16))
        acc = g * p if acc is None else acc + g * p
    o_ref[...] = acc.astype(o_ref.dtype)


def gate_merge(xn, branches, w_gate, b_gate, w_branch, layer, tm, tn):
    n, d = xn.shape
    nb = d // tn
    db = branches[0].shape[1]
    b3 = b_gate.reshape(b_gate.shape[0], 1, b_gate.shape[1])
    row = pl.BlockSpec((tm, d), lambda i, j: (i, 0))
    br = pl.BlockSpec((tm, db), lambda i, j: (i, 0))
    wg = [pl.BlockSpec((None, d, tn), functools.partial(lambda i, j, m: (layer, 0, m * nb + j), m=m))
          for m in range(4)]
    bg = [pl.BlockSpec((None, 1, tn), functools.partial(lambda i, j, m: (layer, 0, m * nb + j), m=m))
          for m in range(4)]
    wb = pl.BlockSpec((None, 4, db, tn), lambda i, j: (layer, 0, 0, j))
    return pl.pallas_call(
        _gate_merge_kernel,
        grid=(n // tm, nb),
        in_specs=[row, br, br, br, br] + wg + bg + [wb],
        out_specs=pl.BlockSpec((tm, tn), lambda i, j: (i, j)),
        out_shape=jax.ShapeDtypeStruct((n, d), BF16),
        compiler_params=_cparams("parallel", "arbitrary"),
        name="gate_merge",
    )(xn, *branches, w_gate, w_gate, w_gate, w_gate, b3, b3, b3, b3, w_branch)


def _glu_kernel(h_ref, w1_ref, w3_ref, o_ref):
    h = h_ref[...]
    a = _dot(h, w1_ref[...].astype(BF16))
    b = _dot(h, w3_ref[...].astype(BF16))
    o_ref[...] = (_silu(a) * b).astype(o_ref.dtype)


def dense_glu(h, w1, w3, idx, tm, tn):
    n, d = h.shape
    dff = w1.shape[-1]
    return pl.pallas_call(
        _glu_kernel,
        grid=(n // tm, dff // tn),
        in_specs=[pl.BlockSpec((tm, d), lambda i, j: (i, 0)),
                  pl.BlockSpec((None, d, tn), lambda i, j: (idx, 0, j)),
                  pl.BlockSpec((None, d, tn), lambda i, j: (idx, 0, j))],
        out_specs=pl.BlockSpec((tm, tn), lambda i, j: (i, j)),
        out_shape=jax.ShapeDtypeStruct((n, dff), BF16),
        compiler_params=_cparams("parallel", "arbitrary"),
        name="dense_glu",
    )(h, w1, w3)


def _row_dma_loop(trips, copies, start):
    def body(g, c):
        for u in range(DMA_UNROLL):
            for cp in copies(g * DMA_UNROLL + u):
                if start:
                    cp.start()
                else:
                    cp.wait()
        return c
    lax.fori_loop(0, trips, body, 0)


def _grouped_glu_kernel(te_ref, ns_ref, tok_ref, h_hbm, w1_ref, w3_ref, o_ref,
                        g_buf, hb_ref, sem, *, sub):
    del te_ref
    n_sub = ns_ref[pl.program_id(0)]
    g_rows = g_buf.shape[0]
    subs_per_pass = g_rows // sub

    for g in range(hb_ref.shape[0] // g_rows):
        live_here = jnp.clip(n_sub - g * subs_per_pass, 0, subs_per_pass)

        @pl.when((live_here > 0) & (pl.program_id(1) == 0))
        def _():
            def copies(i):
                return (pltpu.make_async_copy(h_hbm.at[pl.ds(tok_ref[0, 0, g * g_rows + i], 1), :],
                                              g_buf.at[pl.ds(i, 1), :], sem),)

            _row_dma_loop(live_here * (sub // DMA_UNROLL), copies, start=True)
            _row_dma_loop(live_here * (sub // DMA_UNROLL), copies, start=False)
            for m in range(subs_per_pass):
                @pl.when(m < live_here)
                def _():
                    hb_ref[pl.ds(g * g_rows + m * sub, sub), :] = g_buf[pl.ds(m * sub, sub), :].astype(BF16)

    n_subs = hb_ref.shape[0] // sub
    full = n_sub == n_subs

    @pl.when(full)
    def _():
        _glu_kernel(hb_ref, w1_ref, w3_ref, o_ref)

    @pl.when(jnp.logical_not(full))
    def _():
        w1 = w1_ref[...].astype(BF16)
        w3 = w3_ref[...].astype(BF16)
        for m in range(n_subs):
            rows = pl.ds(m * sub, sub)

            @pl.when(m < n_sub)
            def _():
                h = hb_ref[rows, :]
                o_ref[rows, :] = (_silu(_dot(h, w1)) * _dot(h, w3)).astype(o_ref.dtype)

            @pl.when(m >= n_sub)
            def _():
                o_ref[rows, :] = jnp.zeros((sub, o_ref.shape[1]), o_ref.dtype)


def _expert_block(idx, n_blocks, tiles_per_entry):
    def index_map(r, j, te, ns):
        e = r // tiles_per_entry
        live = ns[e] > (r % tiles_per_entry) * (MOE_SUBS // tiles_per_entry)
        return (idx, te[e], 0, jnp.where(live, j, n_blocks - 1))
    return index_map


def grouped_glu(h32, row_token, tile_expert, n_sub, w1, w3, idx, tn):
    d = h32.shape[1]
    p = row_token.shape[0]
    nr = p // MOE_TILE
    dff = w1.shape[-1]
    nj = dff // tn
    w_spec = pl.BlockSpec((None, None, d, tn), _expert_block(idx, nj, 1))
    grid_spec = pltpu.PrefetchScalarGridSpec(
        num_scalar_prefetch=2,
        grid=(nr, nj),
        in_specs=[pl.BlockSpec((1, 1, MOE_TILE), lambda r, j, te, ns: (r, 0, 0),
                               memory_space=pltpu.SMEM),
                  ANY_SPEC, w_spec, w_spec],
        out_specs=pl.BlockSpec((MOE_TILE, tn), lambda r, j, te, ns: (r, j)),
        scratch_shapes=[pltpu.VMEM((MOE_TILE // 2, d), F32), pltpu.VMEM((MOE_TILE, d), BF16),
                        pltpu.SemaphoreType.DMA],
    )
    return pl.pallas_call(
        functools.partial(_grouped_glu_kernel, sub=MOE_TILE // MOE_SUBS),
        grid_spec=grid_spec,
        out_shape=jax.ShapeDtypeStruct((p, dff), BF16),
        compiler_params=_cparams("arbitrary", "arbitrary"),
        name="grouped_glu",
    )(tile_expert, n_sub, row_token.reshape(nr, 1, MOE_TILE), h32, w1, w3)


def _grouped_down_kernel(te_ref, ns_ref, a_ref, w2_ref, o_ref, *, tiles_per_entry):
    del te_ref
    r = pl.program_id(0)
    live = ns_ref[r // tiles_per_entry] > (r % tiles_per_entry) * (MOE_SUBS // tiles_per_entry)

    @pl.when(live)
    def _():
        o_ref[...] = _dot(a_ref[...], w2_ref[...].astype(BF16))

    @pl.when(jnp.logical_not(live))
    def _():
        o_ref[...] = jnp.zeros_like(o_ref)


def grouped_down(act, tile_expert, n_sub, w2, idx, tm, tn):
    p, dff = act.shape
    d = w2.shape[-1]
    nj = d // tn
    per = MOE_TILE // tm
    grid_spec = pltpu.PrefetchScalarGridSpec(
        num_scalar_prefetch=2,
        grid=(p // tm, nj),
        in_specs=[pl.BlockSpec((tm, dff), lambda r, j, te, ns: (r, 0)),
                  pl.BlockSpec((None, None, dff, tn), _expert_block(idx, nj, per))],
        out_specs=pl.BlockSpec((tm, tn), lambda r, j, te, ns: (r, j)),
    )
    return pl.pallas_call(
        functools.partial(_grouped_down_kernel, tiles_per_entry=per),
        grid_spec=grid_spec,
        out_shape=jax.ShapeDtypeStruct((p, d), F32),
        compiler_params=_cparams("arbitrary", "arbitrary"),
        name="grouped_down",
    )(tile_expert, n_sub, act, w2)


def _combine_kernel(d1_ref, d2_ref, x_ref, r_ref, g_ref, yg_hbm, yp_ref, ys_ref, y1_buf, y2_buf, sem,
                    *, tc, np_tiles):
    def copies(t):
        return (pltpu.make_async_copy(yg_hbm.at[pl.ds(d1_ref[0, 0, t], 1), :],
                                      y1_buf.at[pl.ds(t, 1), :], sem),
                pltpu.make_async_copy(yg_hbm.at[pl.ds(d2_ref[0, 0, t], 1), :],
                                      y2_buf.at[pl.ds(t, 1), :], sem))

    _row_dma_loop(tc // DMA_UNROLL, copies, start=True)
    _row_dma_loop(tc // DMA_UNROLL, copies, start=False)
    route = r_ref[...]
    i1 = route[:, 0:1]
    i2 = route[:, 1:2]
    w1 = route[:, 2:3]
    w2 = route[:, 3:4]
    first_is_low = i1 < i2
    ya = jnp.where(first_is_low, w1 * y1_buf[...], w2 * y2_buf[...])
    yb = jnp.where(first_is_low, w2 * y2_buf[...], w1 * y1_buf[...])
    y = _rmsnorm_rows(x_ref[...] + (ya + yb), g_ref[...])
    is_prompt = pl.program_id(0) < np_tiles

    @pl.when(is_prompt)
    def _():
        yp_ref[...] = y

    @pl.when(jnp.logical_not(is_prompt))
    def _():
        ys_ref[...] = y


def moe_combine_norm(x, route, yg, d1, d2, g, n_p, tc):
    n, d = x.shape
    nc = n // tc
    np_tiles = n_p // tc
    idx = pl.BlockSpec((1, 1, tc), lambda c: (c, 0, 0), memory_space=pltpu.SMEM)
    return pl.pallas_call(
        functools.partial(_combine_kernel, tc=tc, np_tiles=np_tiles),
        grid=(nc,),
        in_specs=[idx, idx, pl.BlockSpec((tc, d), lambda c: (c, 0)),
                  pl.BlockSpec((tc, ROUTER_LANES), lambda c: (c, 0)),
                  pl.BlockSpec((1, d), lambda c: (0, 0)), ANY_SPEC],
        out_specs=[pl.BlockSpec((tc, d), lambda c: (jnp.minimum(c, np_tiles - 1), 0)),
                   pl.BlockSpec((tc, d), lambda c: (jnp.maximum(c - np_tiles, 0), 0))],
        out_shape=[jax.ShapeDtypeStruct((n_p, d), F32), jax.ShapeDtypeStruct((n - n_p, d), F32)],
        scratch_shapes=[pltpu.VMEM((tc, d), F32), pltpu.VMEM((tc, d), F32),
                        pltpu.SemaphoreType.DMA],
        compiler_params=_cparams("arbitrary"),
        name="moe_combine_norm",
    )(d1.reshape(nc, 1, tc), d2.reshape(nc, 1, tc), x, route, g.reshape(1, d), yg)


def moe_plan(route):
    n = route.shape[0]
    sub = MOE_TILE // MOE_SUBS
    i1 = route[:, 0].astype(jnp.int32)
    i2 = route[:, 1].astype(jnp.int32)
    experts = jnp.arange(N_EXPERTS, dtype=jnp.int32)
    onehot = ((i1[:, None] == experts) | (i2[:, None] == experts)).astype(jnp.int32)
    counts = jnp.sum(onehot, axis=0)
    subs = (counts + sub - 1) // sub
    tiles = (counts + MOE_TILE - 1) // MOE_TILE
    tile_end = jnp.cumsum(tiles)
    tile_start = tile_end - tiles
    pos = jnp.cumsum(onehot, axis=0) - onehot
    dest = (tile_start * MOE_TILE)[None, :] + pos
    d1 = jnp.sum(jnp.where(i1[:, None] == experts, dest, 0), axis=1)
    d2 = jnp.sum(jnp.where(i2[:, None] == experts, dest, 0), axis=1)
    n_tiles = -(-(2 * n + N_EXPERTS * (MOE_TILE - 1)) // MOE_TILE)
    r = jnp.arange(n_tiles, dtype=jnp.int32)
    owner = jnp.minimum(jnp.sum((r[:, None] >= tile_end[None, :]).astype(jnp.int32), axis=1),
                        N_EXPERTS - 1)
    is_owner = owner[:, None] == experts[None, :]
    first_tile = jnp.sum(jnp.where(is_owner, tile_start[None, :], 0), axis=1)
    owner_subs = jnp.sum(jnp.where(is_owner, subs[None, :], 0), axis=1)
    n_used = tile_end[-1]
    n_sub = jnp.where(r < n_used, jnp.clip(owner_subs - (r - first_tile) * MOE_SUBS, 0, MOE_SUBS), 0)
    last_expert = jnp.sum(jnp.where(r == n_used - 1, owner, 0))
    tile_expert = jnp.where(r < n_used, owner, last_expert).astype(jnp.int32)
    tokens = jnp.arange(n, dtype=jnp.int32)
    row_token = jnp.zeros((n_tiles * MOE_TILE,), jnp.int32).at[jnp.concatenate([d1, d2])].set(
        jnp.concatenate([tokens, tokens]), unique_indices=True)
    return d1, d2, row_token, tile_expert, n_sub.astype(jnp.int32)


def _hgrn_kernel(p_ref, lb_ref, hn_ref, s0_ref, out_hbm, oa_ref, sout_ref,
                 st_ref, q_s, k_s, v_s, b_s, o_s, *, tt):
    del out_hbm
    c_len = HGRN_CHUNK
    t = pl.program_id(1)

    @pl.when(t == 0)
    def _():
        for h in range(N_HEADS):
            st_ref[h] = s0_ref[0, h].T

    lb = lb_ref[...]
    keep = 1.0 - lb
    z = p_ref[:, D_BRANCH:2 * D_BRANCH]
    e = jnp.exp(-jnp.abs(z))
    r = 1.0 / (1.0 + e)
    er = e * r
    logf = jnp.log(jnp.maximum(lb, LB_FLOOR) + keep * jnp.where(z >= 0.0, r, er))
    k_s[...] = keep * jnp.where(z >= 0.0, er, r)
    q_s[...] = _silu(p_ref[:, 0:D_BRANCH])
    v_s[...] = p_ref[:, 2 * D_BRANCH:3 * D_BRANCH]
    row = lax.broadcasted_iota(jnp.int32, (tt, tt), 0)
    col = lax.broadcasted_iota(jnp.int32, (tt, tt), 1)
    tri = jnp.where(((row ^ col) < c_len) & (col <= row), 1.0, 0.0).astype(BF16)
    lf_hi, lf_lo = _split_bf16(logf * LOG2_E)
    b_s[...] = _dot(tri, lf_hi) + _dot(tri, lf_lo)

    ones = jnp.ones((D_HEAD, D_HEAD), BF16)
    row_c = lax.broadcasted_iota(jnp.int32, (c_len, 1), 0)

    def chunk(c, carry):
        r0 = pl.multiple_of(c * c_len, c_len)
        qc = q_s[pl.ds(r0, c_len), :]
        kc = k_s[pl.ds(r0, c_len), :]
        vc = v_s[pl.ds(r0, c_len), :]
        bc = b_s[pl.ds(r0, c_len), :]
        b_last = bc[c_len - 1:c_len, :]
        qt = (qc * jnp.exp2(bc)).astype(BF16)
        kt = (kc * jnp.exp2(b_last - bc)).astype(BF16)
        decay = jnp.exp2(b_last)
        vb = vc.astype(BF16)
        xs = []
        for s in range(c_len):
            diff = jnp.where(row_c >= s, bc - bc[s:s + 1, :], NEG_BIG)
            xs.append((qc * kc[s:s + 1, :] * jnp.exp2(diff)).astype(BF16))
        x_all = jnp.concatenate(xs, axis=0)
        heads = [slice(h * D_HEAD, (h + 1) * D_HEAD) for h in range(N_HEADS)]
        states = [st_ref[h] for h in range(N_HEADS)]
        carried = [_dot_nt(qt[:, hs], st.astype(BF16)) for hs, st in zip(heads, states)]
        updates = [_dot_tn(vb[:, hs], kt[:, hs]) for hs in heads]
        sums = [_dot(x_all[:, hs], ones) for hs in heads]
        outs = []
        for hs, acc, y in zip(heads, carried, sums):
            for s in range(c_len):
                acc = acc + y[s * c_len:(s + 1) * c_len, :] * vc[s:s + 1, hs]
            outs.append(acc)
        for h, (hs, st, up) in enumerate(zip(heads, states, updates)):
            st_ref[h] = st * decay[:, hs] + up
        o_s[pl.ds(r0, c_len), :] = jnp.concatenate(outs, axis=1)
        return carry

    lax.fori_loop(0, tt // c_len, chunk, 0, unroll=HGRN_UNROLL)

    gate = _silu(p_ref[:, 3 * D_BRANCH:4 * D_BRANCH])
    hn = hn_ref[...]
    for h in range(N_HEADS):
        hs = slice(h * D_HEAD, (h + 1) * D_HEAD)
        oh = o_s[:, hs]
        oh = oh * lax.rsqrt(jnp.mean(oh * oh, axis=-1, keepdims=True) + EPS) * hn
        oa_ref[:, hs] = (oh * gate[:, hs]).astype(oa_ref.dtype)

    @pl.when(t == pl.num_programs(1) - 1)
    def _():
        for h in range(N_HEADS):
            sout_ref[0, h] = st_ref[h].T


def branch_buffer(n_rows):
    return jnp.zeros((n_rows, D_BRANCH), BF16)


def hgrn_mixer(proj, lb, hnorm, s0, row0, n_seq, seq_len, tt, out):
    nt = seq_len // tt
    rb0 = row0 // tt
    scr = [pltpu.VMEM((N_HEADS, D_HEAD, D_HEAD), F32)] + [pltpu.VMEM((tt, D_BRANCH), F32)] * 5
    return pl.pallas_call(
        functools.partial(_hgrn_kernel, tt=tt),
        grid=(n_seq, nt),
        in_specs=[pl.BlockSpec((tt, 4 * D_BRANCH), lambda s, t: (rb0 + s * nt + t, 0)),
                  pl.BlockSpec((1, D_BRANCH), lambda s, t: (0, 0)),
                  pl.BlockSpec((1, D_HEAD), lambda s, t: (0, 0)),
                  pl.BlockSpec((1, N_HEADS, D_HEAD, D_HEAD), lambda s, t: (s, 0, 0, 0)), ANY_SPEC],
        out_specs=[pl.BlockSpec((tt, D_BRANCH), lambda s, t: (rb0 + s * nt + t, 0)),
                   pl.BlockSpec((1, N_HEADS, D_HEAD, D_HEAD), lambda s, t: (s, 0, 0, 0))],
        out_shape=[jax.ShapeDtypeStruct(out.shape, out.dtype),
                   jax.ShapeDtypeStruct((n_seq, N_HEADS, D_HEAD, D_HEAD), F32)],
        scratch_shapes=scr,
        input_output_aliases={4: 0},
        compiler_params=_cparams("arbitrary", "arbitrary"),
        name="hgrn_mixer",
    )(proj, lb.reshape(1, D_BRANCH), hnorm.reshape(1, D_HEAD), s0, out)


def _pool_conv_kernel(p_ref, hp_ref, hc_ref, wp_ref, ps_ref, cw_ref, ob_hbm, oc_hbm,
                      ob_ref, oc_ref, pn_ref, cn_ref, xe, ue, *, tt, pos0):
    del ob_hbm, oc_hbm
    t = pl.program_id(1)

    @pl.when(t == 0)
    def _():
        xe[0:POOL_HALO, :] = hp_ref[0]
        ue[0:CONV_HALO, :] = hc_ref[0]

    @pl.when(t > 0)
    def _():
        xe[0:POOL_HALO, :] = xe[tt:tt + POOL_HALO, :]
        ue[0:CONV_HALO, :] = ue[tt:tt + CONV_HALO, :]

    x = p_ref[:, 0:D_BRANCH]
    xe[POOL_HALO:POOL_HALO + tt, :] = x
    u = p_ref[:, 3 * D_BRANCH:4 * D_BRANCH] * p_ref[:, D_BRANCH:2 * D_BRANCH]
    ue[CONV_HALO:CONV_HALO + tt, :] = u

    pos = pos0 + t * tt + lax.broadcasted_iota(jnp.int32, (tt, 1), 0)
    group = D_BRANCH // len(POOL_WINDOWS)
    for g, w in enumerate(POOL_WINDOWS):
        ls = slice(g * group, (g + 1) * group)
        s = x[:, ls]
        for j in range(1, w):
            s = s + xe[POOL_HALO - j:POOL_HALO - j + tt, ls]
        cnt = jnp.minimum(pos + 1, w).astype(F32)
        dlt = s / cnt - x[:, ls]
        y = _dot(dlt.astype(BF16), wp_ref[g].astype(BF16))
        ob_ref[:, ls] = (y * ps_ref[:, ls]).astype(ob_ref.dtype)

    y = ue[CONV_HALO - 2:CONV_HALO - 2 + tt, :] * cw_ref[0:1, :]
    for j in range(1, CONV_W):
        y = y + ue[CONV_HALO - 2 + j:CONV_HALO - 2 + j + tt, :] * cw_ref[j:j + 1, :]
    oc_ref[...] = (p_ref[:, 2 * D_BRANCH:3 * D_BRANCH] * y).astype(oc_ref.dtype)

    pn_ref[0] = xe[tt:tt + POOL_HALO, :]
    cn_ref[0] = ue[tt:tt + CONV_HALO, :]


def pool_conv_mixer(proj, hist_pool, hist_conv, w_pool, pool_scale, conv_w,
                    row0, n_seq, seq_len, tt, pos0, outs):
    nt = seq_len // tt
    rb0 = row0 // tt
    out_shape = jax.ShapeDtypeStruct(outs[0].shape, outs[0].dtype)
    cw = jnp.zeros((8, D_BRANCH), F32).at[:CONV_W].set(conv_w)
    out_row = pl.BlockSpec((tt, D_BRANCH), lambda s, t: (rb0 + s * nt + t, 0))
    return pl.pallas_call(
        functools.partial(_pool_conv_kernel, tt=tt, pos0=pos0),
        grid=(n_seq, nt),
        in_specs=[pl.BlockSpec((tt, 4 * D_BRANCH), lambda s, t: (rb0 + s * nt + t, 1)),
                  pl.BlockSpec((1, POOL_HALO, D_BRANCH), lambda s, t: (s, 0, 0)),
                  pl.BlockSpec((1, CONV_HALO, D_BRANCH), lambda s, t: (s, 0, 0)),
                  pl.BlockSpec(w_pool.shape, lambda s, t: (0, 0, 0)),
                  pl.BlockSpec((1, D_BRANCH), lambda s, t: (0, 0)),
                  pl.BlockSpec((8, D_BRANCH), lambda s, t: (0, 0)), ANY_SPEC, ANY_SPEC],
        out_specs=[out_row, out_row,
                   pl.BlockSpec((1, POOL_HALO, D_BRANCH), lambda s, t: (s, 0, 0)),
                   pl.BlockSpec((1, CONV_HALO, D_BRANCH), lambda s, t: (s, 0, 0))],
        out_shape=[out_shape, out_shape,
                   jax.ShapeDtypeStruct((n_seq, POOL_HALO, D_BRANCH), F32),
                   jax.ShapeDtypeStruct((n_seq, CONV_HALO, D_BRANCH), F32)],
        scratch_shapes=[pltpu.VMEM((tt + POOL_HALO, D_BRANCH), F32),
                        pltpu.VMEM((tt + CONV_HALO, D_BRANCH), F32)],
        input_output_aliases={6: 0, 7: 1},
        compiler_params=_cparams("arbitrary", "arbitrary"),
        name="pool_conv_mixer",
    )(proj, hist_pool, hist_conv, w_pool, pool_scale.reshape(1, D_BRANCH), cw, *outs)


def _sb_tile(q_ref, kv, o_acc, run, masked):
    tq = q_ref.shape[0]
    scale = D_HEAD ** -0.5
    kr = lax.broadcasted_iota(jnp.int32, (SB_TILE, 2 * SB_TILE), 0)
    kc = lax.broadcasted_iota(jnp.int32, (SB_TILE, 2 * SB_TILE), 1)
    sums = jnp.where((kc >= SB_TILE) | (kr > kc), 1.0, 0.0).astype(BF16)
    if masked:
        qi = lax.broadcasted_iota(jnp.int32, (tq, SB_TILE), 0)
        ki = lax.broadcasted_iota(jnp.int32, (tq, SB_TILE), 1)
        mask = ki < qi
    run_old = run[...]
    heads = [slice(h * D_HEAD, (h + 1) * D_HEAD) for h in range(N_HEADS)]
    kvs = [kv(h) for h in range(N_HEADS)]
    zs = [_dot_nt(q_ref[:, hs], k.astype(BF16)) * scale for hs, (k, _) in zip(heads, kvs)]
    lks = []
    for z in zs:
        log_keep = -(jnp.maximum(z, 0.0) + jnp.log1p(jnp.exp(-jnp.abs(z))))
        lks.append(jnp.where(mask, log_keep, 0.0) if masked else log_keep)
    splits = [_split_bf16(lk) for lk in lks]
    sms = [_dot(hi, sums) + _dot(lo, sums) for hi, lo in splits]
    probs = []
    for hs, z, lk, sm in zip(heads, zs, lks, sms):
        a = jnp.exp(z + lk + sm[:, 0:SB_TILE] + run_old[:, hs])
        probs.append((jnp.where(mask, a, 0.0) if masked else a).astype(BF16))
    o_new = [_dot(a, v.astype(BF16)) for a, (_, v) in zip(probs, kvs)]
    o_acc[...] += jnp.concatenate(o_new, axis=1)
    run[...] = run_old + jnp.concatenate([sm[:, SB_TILE:2 * SB_TILE] for sm in sms], axis=1)


def _sb_alive(run):
    return (jnp.max(run[...]) > SB_DONE).astype(jnp.int32)


def _sb_past_loop(q_ref, kv_tile, o_acc, run, n_tiles):
    def cond(c):
        j, alive = c
        return (j >= 0) & (alive > 0)

    def body(c):
        j, _ = c
        r0 = pl.multiple_of(j * SB_TILE, SB_TILE)
        _sb_tile(q_ref, functools.partial(kv_tile, r0), o_acc, run, masked=False)
        return j - 1, _sb_alive(run)

    lax.while_loop(cond, body, (jnp.int32(n_tiles) - 1, _sb_alive(run)))


def _sb_prompt_kernel(q_ref, k_ref, v_ref, out_hbm, o_ref, o_acc, run):
    del out_hbm
    i = pl.program_id(1)
    o_acc[...] = jnp.zeros_like(o_acc)
    run[...] = jnp.zeros_like(run)

    def kv_tile(r0, h):
        hs = slice(h * D_HEAD, (h + 1) * D_HEAD)
        return k_ref[pl.ds(r0, SB_TILE), hs], v_ref[pl.ds(r0, SB_TILE), hs]

    _sb_tile(q_ref, functools.partial(kv_tile, pl.multiple_of(i * SB_TILE, SB_TILE)), o_acc, run,
             masked=True)
    _sb_past_loop(q_ref, kv_tile, o_acc, run, i)
    o_ref[...] = o_acc[...].astype(o_ref.dtype)


def sb_prompt_mixer(qkv, n_seq, seq_len, out):
    nq = seq_len // SB_TILE
    seq = lambda c: pl.BlockSpec((seq_len, D_BRANCH), lambda s, i: (s, c))
    return pl.pallas_call(
        _sb_prompt_kernel,
        grid=(n_seq, nq),
        in_specs=[pl.BlockSpec((SB_TILE, D_BRANCH), lambda s, i: (s * nq + i, 0)), seq(1), seq(2),
                  ANY_SPEC],
        out_specs=pl.BlockSpec((SB_TILE, D_BRANCH), lambda s, i: (s * nq + i, 0)),
        out_shape=jax.ShapeDtypeStruct(out.shape, out.dtype),
        scratch_shapes=[pltpu.VMEM((SB_TILE, D_BRANCH), F32), pltpu.VMEM((SB_TILE, D_BRANCH), F32)],
        input_output_aliases={3: 0},
        compiler_params=_cparams("arbitrary", "arbitrary"),
        name="sb_prompt_mixer",
    )(qkv, qkv, qkv, out)


def _sb_sample_kernel(qkv_ref, kp_ref, vp_ref, out_hbm, o_ref, o_acc, run, *, n_past_tiles):
    del out_hbm
    o_acc[...] = jnp.zeros_like(o_acc)
    run[...] = jnp.zeros_like(run)
    tq = qkv_ref.shape[0]
    q_ref = qkv_ref.at[:, 0:D_BRANCH]
    pad = jnp.zeros((SB_TILE - tq, D_BRANCH), qkv_ref.dtype)
    k_new = jnp.concatenate([qkv_ref[:, D_BRANCH:2 * D_BRANCH], pad], axis=0)
    v_new = jnp.concatenate([qkv_ref[:, 2 * D_BRANCH:3 * D_BRANCH], pad], axis=0)

    def kv_new(h):
        hs = slice(h * D_HEAD, (h + 1) * D_HEAD)
        return k_new[:, hs], v_new[:, hs]

    def kv_past(r0, h):
        return kp_ref[pl.ds(r0, SB_TILE), h, :], vp_ref[pl.ds(r0, SB_TILE), h, :]

    _sb_tile(q_ref, kv_new, o_acc, run, masked=True)
    _sb_past_loop(q_ref, kv_past, o_acc, run, n_past_tiles)
    o_ref[...] = o_acc[...].astype(o_ref.dtype)


def sb_sample_mixer(qkv, k_cache, v_cache, layer, row0, n_seq, seq_len, out):
    past = k_cache.shape[2]
    rb0 = row0 // seq_len
    cache = pl.BlockSpec((None, None, past, N_HEADS, D_HEAD), lambda s: (layer, s, 0, 0, 0))
    return pl.pallas_call(
        functools.partial(_sb_sample_kernel, n_past_tiles=past // SB_TILE),
        grid=(n_seq,),
        in_specs=[pl.BlockSpec((seq_len, 3 * D_BRANCH), lambda s: (rb0 + s, 0)), cache, cache, ANY_SPEC],
        out_specs=pl.BlockSpec((seq_len, D_BRANCH), lambda s: (rb0 + s, 0)),
        out_shape=jax.ShapeDtypeStruct(out.shape, out.dtype),
        scratch_shapes=[pltpu.VMEM((seq_len, D_BRANCH), F32), pltpu.VMEM((seq_len, D_BRANCH), F32)],
        input_output_aliases={3: 0},
        compiler_params=_cparams("arbitrary"),
        name="sb_sample_mixer",
    )(qkv, k_cache, v_cache, out)


def _tile(n, want):
    if n <= want:
        return n
    for t in range(want, 7, -8):
        if n % t == 0:
            return t
    return n


def kernel(x_prompt, x_sample, state_hgrn, state_pool, state_conv, cache_k, cache_v, norm_mix, norm_ffn, norm_final, w_in, hgrn_lower_bound, hgrn_out_norm, pool_w, pool_scale, conv_w, w_branch, w_gate, b_gate, w_out, ffn_w1, ffn_w3, ffn_w2, moe_router, moe_w1, moe_w3, moe_w2):
    depth = w_in.shape[0]
    pb, pt, d = x_prompt.shape
    sb, st, _ = x_sample.shape
    past = cache_k.shape[2]
    n_p = pb * pt
    n_s = sb * st
    n = n_p + n_s
    assert depth % 2 == 0, "the last layer is expected to be a MoE layer"
    x = (x_prompt.reshape(n_p, d), x_sample.reshape(n_s, d))

    tm_row = _tile(n_s, 512)
    tm_mm = _tile(n_s, 2048)
    tm_half = tm_mm // 2
    tc = _tile(n_s, 256)
    tt_p = _tile(pt, 256)
    assert n_p % tm_mm == 0 and n_p % tc == 0 and n_p % tm_row == 0

    sm = jax.nn.softmax(hgrn_lower_bound.astype(F32), axis=0)
    lbs = jnp.cumsum(sm, axis=0) - sm[0:1]
    ffn_w2_bf16 = ffn_w2.astype(BF16)

    zeros_state = jnp.zeros((pb, N_HEADS, D_HEAD, D_HEAD), F32)
    zeros_pool = jnp.zeros((pb, POOL_HALO, D_BRANCH), F32)
    zeros_conv = jnp.zeros((pb, CONV_HALO, D_BRANCH), F32)
    pool_pad = ((0, 0), (0, 0), (POOL_HALO - state_pool.shape[2], 0), (0, 0))
    conv_pad = ((0, 0), (0, 0), (CONV_HALO - state_conv.shape[2], 0), (0, 0))
    sample_pool = jnp.pad(state_pool, pool_pad)
    sample_conv = jnp.pad(state_conv, conv_pad)

    outs = {k: [] for k in ("p_h", "p_pool", "p_conv", "s_h", "s_pool", "s_conv")}
    caches = cache_buffers(depth, n_p, n_s)
    xn = rmsnorm(x, norm_mix[0], BF16, tm_row)
    y_p = y_s = None
    for l in range(depth):
        proj = in_proj(xn, w_in, l, 8 * D_BRANCH, tm_mm, D_BRANCH)
        qkv16, *caches = qkv_proj(xn, w_in, l, 8 * D_BRANCH, n_p, caches, tm_half)

        oa, hs_p = hgrn_mixer(proj, lbs[l], hgrn_out_norm[l], zeros_state, 0, pb, pt, tt_p,
                              branch_buffer(n))
        oa, hs_s = hgrn_mixer(proj, lbs[l], hgrn_out_norm[l], state_hgrn[l], n_p, sb, st, st, oa)
        ob, oc, pool_p, conv_p = pool_conv_mixer(
            proj, zeros_pool, zeros_conv, pool_w[l], pool_scale[l], conv_w[l], 0, pb, pt, tt_p, 0,
            (branch_buffer(n), branch_buffer(n)))
        ob, oc, pool_s, conv_s = pool_conv_mixer(
            proj, sample_pool[l], sample_conv[l], pool_w[l], pool_scale[l], conv_w[l],
            n_p, sb, st, st, past, (ob, oc))
        od = sb_prompt_mixer(qkv16, pb, pt, branch_buffer(n))
        od = sb_sample_mixer(qkv16, cache_k, cache_v, l, n_p, sb, st, od)

        merged = gate_merge(xn, (oa, ob, oc, od), w_gate, b_gate, w_branch, l, tm_half, 256)
        if l % 2 == 0:
            x, h = mix_out(merged, w_out[l].astype(BF16), x, norm_ffn[l], tm_row)
        else:
            x = out_proj(merged, w_out, l, x, tm_mm, D_BRANCH)

        outs["p_h"].append(hs_p)
        outs["s_h"].append(hs_s)
        outs["p_pool"].append(pool_p[:, 1:])
        outs["s_pool"].append(pool_s[:, 1:])
        outs["p_conv"].append(conv_p[:, CONV_HALO - (CONV_W - 1):])
        outs["s_conv"].append(conv_s[:, CONV_HALO - (CONV_W - 1):])

        if l % 2 == 0:
            act = dense_glu(h, ffn_w1, ffn_w3, l // 2, tm_mm, D_BRANCH)
            x = out_proj(act, ffn_w2_bf16, l // 2, x, tm_half, D_BRANCH)
            xn = rmsnorm(x, norm_mix[l + 1], BF16, tm_row)
        else:
            h32, route = rmsnorm_router(x, norm_ffn[l], moe_router[l // 2], tm_row)
            d1, d2, row_token, tile_expert, n_sub = moe_plan(route)
            act = grouped_glu(h32, row_token, tile_expert, n_sub, moe_w1, moe_w3, l // 2,
                              D_BRANCH)
            yg = grouped_down(act, tile_expert, n_sub, moe_w2, l // 2, MOE_TILE // 2, 256)
            if l + 1 < depth:
                raise NotImplementedError("a MoE layer followed by another layer")
            y_p, y_s = moe_combine_norm(x, route, yg, d1, d2, norm_final, n_p, tc)

    pk, pv, sk, sv = caches
    st_ = lambda k: jnp.stack(outs[k])
    return (y_p.reshape(pb, pt, d), y_s.reshape(sb, st, d),
            st_("p_h"), st_("p_pool"), st_("p_conv"),
            pk.reshape(depth, pb, pt, N_HEADS, D_HEAD), pv.reshape(depth, pb, pt, N_HEADS, D_HEAD),
            st_("s_h"), st_("s_pool"), st_("s_conv"),
            sk.reshape(depth, sb, st, N_HEADS, D_HEAD), sv.reshape(depth, sb, st, N_HEADS, D_HEAD))
```

```python
import functools

import jax
import jax.numpy as jnp
from jax import lax
from jax.experimental import pallas as pl
from jax.experimental.pallas import tpu as pltpu

F32 = jnp.float32
BF16 = jnp.bfloat16

EPS = 1e-6
NEG_BIG = -1e30
LB_FLOOR = 1e-30
LOG2_E = 1.4426950408889634
N_HEADS = 4
D_HEAD = 128
D_BRANCH = N_HEADS * D_HEAD
POOL_WINDOWS = (2, 4, 8, 16)
POOL_HALO = 16
CONV_W = 3
CONV_HALO = 8
N_EXPERTS = 8
HGRN_CHUNK = 16
HGRN_UNROLL = 4
SB_TILE = 128
SB_DONE = -120.0
ROUTER_LANES = 128
MOE_TILE = 2048
MOE_SUBS = 4
DMA_UNROLL = 8

V7X_VMEM_LIMIT_BYTES = 56 * 1024 * 1024

ANY_SPEC = pl.BlockSpec(memory_space=pl.ANY)


def _cparams(*sem):
    return pltpu.CompilerParams(dimension_semantics=sem,
                                vmem_limit_bytes=V7X_VMEM_LIMIT_BYTES)


def _split_bf16(x):
    hi = x.astype(BF16)
    lo = (x - hi.astype(F32)).astype(BF16)
    return hi, lo


def _dot(a, b):
    return jnp.dot(a, b, preferred_element_type=F32)


def _dot_nt(a, b):
    return lax.dot_general(a, b, (((1,), (1,)), ((), ())), preferred_element_type=F32)


def _dot_tn(a, b):
    return lax.dot_general(a, b, (((0,), (0,)), ((), ())), preferred_element_type=F32)


def _silu(a):
    return a * jax.nn.sigmoid(a)


def _rmsnorm_rows(x, g):
    return x * lax.rsqrt(jnp.mean(x * x, axis=-1, keepdims=True) + EPS) * g


def _two_source_specs(xs, tm, width, col_of):
    def index_map(i, *j, t0, nt):
        inside = (i >= t0) & (i < t0 + nt)
        return jnp.clip(i - t0, 0, nt - 1), jnp.where(inside, col_of(*j), 0)

    starts, specs, t0 = [], [], 0
    for x in xs:
        nt = x.shape[0] // tm
        starts.append(t0)
        specs.append(pl.BlockSpec((tm, width), functools.partial(index_map, t0=t0, nt=nt)))
        t0 += nt
    return specs, starts, t0


def _pick_source(refs, starts):
    i = pl.program_id(0)
    x = refs[0][...]
    for ref, t0 in zip(refs[1:], starts[1:]):
        x = jnp.where(i >= t0, ref[...], x)
    return x


def _rms_kernel(*refs, starts):
    g_ref, o_ref = refs[-2:]
    x = _pick_source(refs[:-2], starts)
    o_ref[...] = _rmsnorm_rows(x, g_ref[...]).astype(o_ref.dtype)


def rmsnorm(xs, g, out_dtype, tm):
    xs = xs if isinstance(xs, tuple) else (xs,)
    d = xs[0].shape[1]
    specs, starts, n_tiles = _two_source_specs(xs, tm, d, lambda: 0)
    return pl.pallas_call(
        functools.partial(_rms_kernel, starts=starts),
        grid=(n_tiles,),
        in_specs=specs + [pl.BlockSpec((1, d), lambda i: (0, 0))],
        out_specs=pl.BlockSpec((tm, d), lambda i: (i, 0)),
        out_shape=jax.ShapeDtypeStruct((n_tiles * tm, d), out_dtype),
        compiler_params=_cparams("arbitrary"),
        name="rmsnorm",
    )(*xs, g.reshape(1, d))


def _route(h, r_ref, h32_ref, route_ref):
    h_hi, h_lo = _split_bf16(h)
    h32_ref[...] = h_hi.astype(F32)
    r_hi, r_lo = _split_bf16(r_ref[...])
    logits = _dot(h_hi, r_hi) + (_dot(h_lo, r_hi) + _dot(h_hi, r_lo))
    lane = lax.broadcasted_iota(jnp.int32, logits.shape, 1).astype(F32)
    neg_inf = jnp.float32(-jnp.inf)
    l1 = jnp.where(lane < N_EXPERTS, logits, neg_inf)
    m1 = jnp.max(l1, axis=-1, keepdims=True)
    i1 = jnp.min(jnp.where(l1 == m1, lane, float(ROUTER_LANES)), axis=-1, keepdims=True)
    l2 = jnp.where(lane == i1, neg_inf, l1)
    m2 = jnp.max(l2, axis=-1, keepdims=True)
    i2 = jnp.min(jnp.where(l2 == m2, lane, float(ROUTER_LANES)), axis=-1, keepdims=True)
    e = jnp.exp(m2 - m1)
    w1 = 1.0 / (1.0 + e)
    w2 = e / (1.0 + e)
    route_ref[...] = jnp.where(lane == 0.0, i1, jnp.where(lane == 1.0, i2, jnp.where(
        lane == 2.0, w1, jnp.where(lane == 3.0, w2, 0.0))))


def _rms_router_kernel(x_ref, g_ref, r_ref, h32_ref, route_ref):
    _route(_rmsnorm_rows(x_ref[...], g_ref[...]), r_ref, h32_ref, route_ref)


def rmsnorm_router(x, g, router, tm):
    n, d = x.shape
    r_pad = jnp.zeros((d, ROUTER_LANES), F32).at[:, :N_EXPERTS].set(router)
    row = pl.BlockSpec((tm, d), lambda i: (i, 0))
    return pl.pallas_call(
        _rms_router_kernel,
        grid=(n // tm,),
        in_specs=[row, pl.BlockSpec((1, d), lambda i: (0, 0)),
                  pl.BlockSpec((d, ROUTER_LANES), lambda i: (0, 0))],
        out_specs=[row, pl.BlockSpec((tm, ROUTER_LANES), lambda i: (i, 0))],
        out_shape=[jax.ShapeDtypeStruct((n, d), F32), jax.ShapeDtypeStruct((n, ROUTER_LANES), F32)],
        compiler_params=_cparams("parallel"),
        name="rmsnorm_router",
    )(x, g.reshape(1, d), r_pad)


def _mix_out_kernel(m_ref, w_ref, *refs, starts):
    g_ref, xo_ref, h_ref = refs[-3:]
    x = _pick_source(refs[:-3], starts) + _dot(m_ref[...], w_ref[...])
    xo_ref[...] = x
    h_ref[...] = _rmsnorm_rows(x, g_ref[...]).astype(h_ref.dtype)


def mix_out(merged, w, xs, g, tm):
    xs = xs if isinstance(xs, tuple) else (xs,)
    n, k = merged.shape
    d = w.shape[1]
    specs, starts, n_tiles = _two_source_specs(xs, tm, d, lambda: 0)
    assert n_tiles * tm == n
    row = pl.BlockSpec((tm, d), lambda i: (i, 0))
    return pl.pallas_call(
        functools.partial(_mix_out_kernel, starts=starts),
        grid=(n_tiles,),
        in_specs=[pl.BlockSpec((tm, k), lambda i: (i, 0)), pl.BlockSpec((k, d), lambda i: (0, 0))]
                 + specs + [pl.BlockSpec((1, d), lambda i: (0, 0))],
        out_specs=[row, row],
        out_shape=[jax.ShapeDtypeStruct((n, d), F32), jax.ShapeDtypeStruct((n, d), BF16)],
        compiler_params=_cparams("arbitrary"),
        name="mix_out",
    )(merged, w, *xs, g.reshape(1, d))


def _proj_kernel(x_ref, w_ref, o_ref):
    o_ref[...] = _dot(x_ref[...], w_ref[...].astype(BF16)).astype(o_ref.dtype)


def in_proj(xn, w_all, layer, n_cols, tm, tn):
    n, k = xn.shape
    return pl.pallas_call(
        _proj_kernel,
        grid=(n // tm, n_cols // tn),
        in_specs=[pl.BlockSpec((tm, k), lambda i, j: (i, 0)),
                  pl.BlockSpec((None, k, tn), lambda i, j: (layer, 0, j))],
        out_specs=pl.BlockSpec((tm, tn), lambda i, j: (i, j)),
        out_shape=jax.ShapeDtypeStruct((n, n_cols), F32),
        compiler_params=_cparams("parallel", "arbitrary"),
        name="in_proj",
    )(xn, w_all)


def _qkv_kernel(x_ref, w_ref, pk_hbm, pv_hbm, sk_hbm, sv_hbm,
                o16_ref, pk_ref, pv_ref, sk_ref, sv_ref, *, np_tiles):
    del pk_hbm, pv_hbm, sk_hbm, sv_hbm
    r = _dot(x_ref[...], w_ref[...].astype(BF16))
    o16_ref[...] = r.astype(o16_ref.dtype)
    is_prompt = pl.program_id(0) < np_tiles
    j = pl.program_id(1)

    def put(dst):
        for h in range(N_HEADS):
            dst[:, h, :] = r[:, h * D_HEAD:(h + 1) * D_HEAD]

    for col, p_ref, s_ref in ((1, pk_ref, sk_ref), (2, pv_ref, sv_ref)):
        pl.when((j == col) & is_prompt)(functools.partial(put, p_ref))
        pl.when((j == col) & jnp.logical_not(is_prompt))(functools.partial(put, s_ref))


def cache_buffers(depth, n_p, n_s):
    return [jnp.zeros((depth * n, N_HEADS, D_HEAD), F32) for n in (n_p, n_p, n_s, n_s)]


def qkv_proj(xn, w_all, layer, col_start, n_p, caches, tm):
    n, k = xn.shape
    n_s = n - n_p
    np_tiles, ns_tiles = n_p // tm, n_s // tm
    cb0 = col_start // D_BRANCH
    head_blk = (tm, N_HEADS, D_HEAD)
    p_spec = pl.BlockSpec(head_blk, lambda i, j: (layer * np_tiles + jnp.minimum(i, np_tiles - 1), 0, 0))
    s_spec = pl.BlockSpec(head_blk, lambda i, j: (layer * ns_tiles + jnp.maximum(i - np_tiles, 0), 0, 0))
    return pl.pallas_call(
        functools.partial(_qkv_kernel, np_tiles=np_tiles),
        grid=(n // tm, 3),
        in_specs=[pl.BlockSpec((tm, k), lambda i, j: (i, 0)),
                  pl.BlockSpec((None, k, D_BRANCH), lambda i, j: (layer, 0, cb0 + j))]
                 + [ANY_SPEC] * 4,
        out_specs=[pl.BlockSpec((tm, D_BRANCH), lambda i, j: (i, j)), p_spec, p_spec, s_spec, s_spec],
        out_shape=[jax.ShapeDtypeStruct((n, 3 * D_BRANCH), BF16)]
                  + [jax.ShapeDtypeStruct(c.shape, c.dtype) for c in caches],
        input_output_aliases={2: 1, 3: 2, 4: 3, 5: 4},
        compiler_params=_cparams("arbitrary", "arbitrary"),
        name="qkv_proj",
    )(xn, w_all, *caches)


def _out_proj_kernel(m_ref, w_ref, *refs, starts):
    o_ref = refs[-1]
    o_ref[...] = _pick_source(refs[:-1], starts) + _dot(m_ref[...], w_ref[...].astype(BF16))


def out_proj(a, w_all, layer, xs, tm, tn):
    xs = xs if isinstance(xs, tuple) else (xs,)
    n, k = a.shape
    d = xs[0].shape[1]
    specs, starts, n_tiles = _two_source_specs(xs, tm, tn, lambda j: j)
    assert n_tiles * tm == n
    return pl.pallas_call(
        functools.partial(_out_proj_kernel, starts=starts),
        grid=(n_tiles, d // tn),
        in_specs=[pl.BlockSpec((tm, k), lambda i, j: (i, 0)),
                  pl.BlockSpec((None, k, tn), lambda i, j: (layer, 0, j))] + specs,
        out_specs=pl.BlockSpec((tm, tn), lambda i, j: (i, j)),
        out_shape=jax.ShapeDtypeStruct((n, d), F32),
        compiler_params=_cparams("arbitrary", "arbitrary"),
        name="out_proj",
    )(a, w_all, *xs)


def _gate_merge_kernel(xn_ref, oa_ref, ob_ref, oc_ref, od_ref,
                       wg0_ref, wg1_ref, wg2_ref, wg3_ref,
                       bg0_ref, bg1_ref, bg2_ref, bg3_ref, wb_ref, o_ref):
    xn = xn_ref[...]
    branches = (oa_ref, ob_ref, oc_ref, od_ref)
    gates = (wg0_ref, wg1_ref, wg2_ref, wg3_ref)
    biases = (bg0_ref, bg1_ref, bg2_ref, bg3_ref)
    acc = None
    for n in range(4):
        g = jax.nn.sigmoid(_dot(xn, gates[n][...].astype(BF16)) + biases[n][...])
        p = _dot(branches[n][...], wb_ref[n].astype(BF16))
        acc = g * p if acc is None else acc + g * p
    o_ref[...] = acc.astype(o_ref.dtype)


def gate_merge(xn, branches, w_gate, b_gate, w_branch, layer, tm, tn):
    n, d = xn.shape
    nb = d // tn
    db = branches[0].shape[1]
    b3 = b_gate.reshape(b_gate.shape[0], 1, b_gate.shape[1])
    row = pl.BlockSpec((tm, d), lambda i, j: (i, 0))
    br = pl.BlockSpec((tm, db), lambda i, j: (i, 0))
    wg = [pl.BlockSpec((None, d, tn), functools.partial(lambda i, j, m: (layer, 0, m * nb + j), m=m))
          for m in range(4)]
    bg = [pl.BlockSpec((None, 1, tn), functools.partial(lambda i, j, m: (layer, 0, m * nb + j), m=m))
          for m in range(4)]
    wb = pl.BlockSpec((None, 4, db, tn), lambda i, j: (layer, 0, 0, j))
    return pl.pallas_call(
        _gate_merge_kernel,
        grid=(n // tm, nb),
        in_specs=[row, br, br, br, br] + wg + bg + [wb],
        out_specs=pl.BlockSpec((tm, tn), lambda i, j: (i, j)),
        out_shape=jax.ShapeDtypeStruct((n, d), BF16),
        compiler_params=_cparams("parallel", "arbitrary"),
        name="gate_merge",
    )(xn, *branches, w_gate, w_gate, w_gate, w_gate, b3, b3, b3, b3, w_branch)


def _glu_kernel(h_ref, w1_ref, w3_ref, o_ref):
    h = h_ref[...]
    a = _dot(h, w1_ref[...].astype(BF16))
    b = _dot(h, w3_ref[...].astype(BF16))
    o_ref[...] = (_silu(a) * b).astype(o_ref.dtype)


def dense_glu(h, w1, w3, idx, tm, tn):
    n, d = h.shape
    dff = w1.shape[-1]
    return pl.pallas_call(
        _glu_kernel,
        grid=(n // tm, dff // tn),
        in_specs=[pl.BlockSpec((tm, d), lambda i, j: (i, 0)),
                  pl.BlockSpec((None, d, tn), lambda i, j: (idx, 0, j)),
                  pl.BlockSpec((None, d, tn), lambda i, j: (idx, 0, j))],
        out_specs=pl.BlockSpec((tm, tn), lambda i, j: (i, j)),
        out_shape=jax.ShapeDtypeStruct((n, dff), BF16),
        compiler_params=_cparams("parallel", "arbitrary"),
        name="dense_glu",
    )(h, w1, w3)


def _row_dma_loop(trips, copies, start):
    def body(g, c):
        for u in range(DMA_UNROLL):
            for cp in copies(g * DMA_UNROLL + u):
                if start:
                    cp.start()
                else:
                    cp.wait()
        return c
    lax.fori_loop(0, trips, body, 0)


def _grouped_glu_kernel(te_ref, ns_ref, tok_ref, h_hbm, w1_ref, w3_ref, o_ref,
                        g_buf, hb_ref, sem, *, sub):
    del te_ref
    n_sub = ns_ref[pl.program_id(0)]
    g_rows = g_buf.shape[0]
    subs_per_pass = g_rows // sub

    for g in range(hb_ref.shape[0] // g_rows):
        live_here = jnp.clip(n_sub - g * subs_per_pass, 0, subs_per_pass)

        @pl.when((live_here > 0) & (pl.program_id(1) == 0))
        def _():
            def copies(i):
                return (pltpu.make_async_copy(h_hbm.at[pl.ds(tok_ref[0, 0, g * g_rows + i], 1), :],
                                              g_buf.at[pl.ds(i, 1), :], sem),)

            _row_dma_loop(live_here * (sub // DMA_UNROLL), copies, start=True)
            _row_dma_loop(live_here * (sub // DMA_UNROLL), copies, start=False)
            for m in range(subs_per_pass):
                @pl.when(m < live_here)
                def _():
                    hb_ref[pl.ds(g * g_rows + m * sub, sub), :] = g_buf[pl.ds(m * sub, sub), :].astype(BF16)

    n_subs = hb_ref.shape[0] // sub
    full = n_sub == n_subs

    @pl.when(full)
    def _():
        _glu_kernel(hb_ref, w1_ref, w3_ref, o_ref)

    @pl.when(jnp.logical_not(full))
    def _():
        w1 = w1_ref[...].astype(BF16)
        w3 = w3_ref[...].astype(BF16)
        for m in range(n_subs):
            rows = pl.ds(m * sub, sub)

            @pl.when(m < n_sub)
            def _():
                h = hb_ref[rows, :]
                o_ref[rows, :] = (_silu(_dot(h, w1)) * _dot(h, w3)).astype(o_ref.dtype)

            @pl.when(m >= n_sub)
            def _():
                o_ref[rows, :] = jnp.zeros((sub, o_ref.shape[1]), o_ref.dtype)


def _expert_block(idx, n_blocks, tiles_per_entry):
    def index_map(r, j, te, ns):
        e = r // tiles_per_entry
        live = ns[e] > (r % tiles_per_entry) * (MOE_SUBS // tiles_per_entry)
        return (idx, te[e], 0, jnp.where(live, j, n_blocks - 1))
    return index_map


def grouped_glu(h32, row_token, tile_expert, n_sub, w1, w3, idx, tn):
    d = h32.shape[1]
    p = row_token.shape[0]
    nr = p // MOE_TILE
    dff = w1.shape[-1]
    nj = dff // tn
    w_spec = pl.BlockSpec((None, None, d, tn), _expert_block(idx, nj, 1))
    grid_spec = pltpu.PrefetchScalarGridSpec(
        num_scalar_prefetch=2,
        grid=(nr, nj),
        in_specs=[pl.BlockSpec((1, 1, MOE_TILE), lambda r, j, te, ns: (r, 0, 0),
                               memory_space=pltpu.SMEM),
                  ANY_SPEC, w_spec, w_spec],
        out_specs=pl.BlockSpec((MOE_TILE, tn), lambda r, j, te, ns: (r, j)),
        scratch_shapes=[pltpu.VMEM((MOE_TILE // 2, d), F32), pltpu.VMEM((MOE_TILE, d), BF16),
                        pltpu.SemaphoreType.DMA],
    )
    return pl.pallas_call(
        functools.partial(_grouped_glu_kernel, sub=MOE_TILE // MOE_SUBS),
        grid_spec=grid_spec,
        out_shape=jax.ShapeDtypeStruct((p, dff), BF16),
        compiler_params=_cparams("arbitrary", "arbitrary"),
        name="grouped_glu",
    )(tile_expert, n_sub, row_token.reshape(nr, 1, MOE_TILE), h32, w1, w3)


def _grouped_down_kernel(te_ref, ns_ref, a_ref, w2_ref, o_ref, *, tiles_per_entry):
    del te_ref
    r = pl.program_id(0)
    subs_here = MOE_SUBS // tiles_per_entry
    sub = o_ref.shape[0] // subs_here
    live = jnp.clip(ns_ref[r // tiles_per_entry] - (r % tiles_per_entry) * subs_here, 0, subs_here)

    @pl.when(live == subs_here)
    def _():
        o_ref[...] = _dot(a_ref[...], w2_ref[...].astype(BF16))

    @pl.when(live < subs_here)
    def _():
        w2 = w2_ref[...].astype(BF16)
        for m in range(subs_here):
            rows = pl.ds(m * sub, sub)

            @pl.when(m < live)
            def _():
                o_ref[rows, :] = _dot(a_ref[rows, :], w2)

            @pl.when(m >= live)
            def _():
                o_ref[rows, :] = jnp.zeros((sub, o_ref.shape[1]), o_ref.dtype)


def grouped_down(act, tile_expert, n_sub, w2, idx, tm, tn):
    p, dff = act.shape
    d = w2.shape[-1]
    nj = d // tn
    per = MOE_TILE // tm
    grid_spec = pltpu.PrefetchScalarGridSpec(
        num_scalar_prefetch=2,
        grid=(p // tm, nj),
        in_specs=[pl.BlockSpec((tm, dff), lambda r, j, te, ns: (r, 0)),
                  pl.BlockSpec((None, None, dff, tn), _expert_block(idx, nj, per))],
        out_specs=pl.BlockSpec((tm, tn), lambda r, j, te, ns: (r, j)),
    )
    return pl.pallas_call(
        functools.partial(_grouped_down_kernel, tiles_per_entry=per),
        grid_spec=grid_spec,
        out_shape=jax.ShapeDtypeStruct((p, d), F32),
        compiler_params=_cparams("arbitrary", "arbitrary"),
        name="grouped_down",
    )(tile_expert, n_sub, act, w2)


def _combine_kernel(d1_ref, d2_ref, d1n_ref, d2n_ref, x_ref, r_ref, g_ref, yg_hbm, yp_ref, ys_ref,
                    y1_buf, y2_buf, sem, *, tc, np_tiles):
    c = pl.program_id(0)
    slot = c % 2

    def copies(i1_ref, i2_ref, s):
        def of_token(t):
            return (pltpu.make_async_copy(yg_hbm.at[pl.ds(i1_ref[0, 0, t], 1), :],
                                          y1_buf.at[s, pl.ds(t, 1), :], sem.at[s]),
                    pltpu.make_async_copy(yg_hbm.at[pl.ds(i2_ref[0, 0, t], 1), :],
                                          y2_buf.at[s, pl.ds(t, 1), :], sem.at[s]))
        return of_token

    @pl.when(c == 0)
    def _():
        _row_dma_loop(tc // DMA_UNROLL, copies(d1_ref, d2_ref, slot), start=True)

    @pl.when(c + 1 < pl.num_programs(0))
    def _():
        _row_dma_loop(tc // DMA_UNROLL, copies(d1n_ref, d2n_ref, 1 - slot), start=True)

    _row_dma_loop(tc // DMA_UNROLL, copies(d1_ref, d2_ref, slot), start=False)
    route = r_ref[...]
    i1 = route[:, 0:1]
    i2 = route[:, 1:2]
    w1 = route[:, 2:3]
    w2 = route[:, 3:4]
    first_is_low = i1 < i2
    y1 = y1_buf[slot]
    y2 = y2_buf[slot]
    ya = jnp.where(first_is_low, w1 * y1, w2 * y2)
    yb = jnp.where(first_is_low, w2 * y2, w1 * y1)
    y = _rmsnorm_rows(x_ref[...] + (ya + yb), g_ref[...])
    is_prompt = c < np_tiles

    @pl.when(is_prompt)
    def _():
        yp_ref[...] = y

    @pl.when(jnp.logical_not(is_prompt))
    def _():
        ys_ref[...] = y


def moe_combine_norm(x, route, yg, d1, d2, g, n_p, tc):
    n, d = x.shape
    nc = n // tc
    np_tiles = n_p // tc
    idx = pl.BlockSpec((1, 1, tc), lambda c: (c, 0, 0), memory_space=pltpu.SMEM)
    idx_next = pl.BlockSpec((1, 1, tc), lambda c: (jnp.minimum(c + 1, nc - 1), 0, 0),
                            memory_space=pltpu.SMEM)
    d1 = d1.reshape(nc, 1, tc)
    d2 = d2.reshape(nc, 1, tc)
    return pl.pallas_call(
        functools.partial(_combine_kernel, tc=tc, np_tiles=np_tiles),
        grid=(nc,),
        in_specs=[idx, idx, idx_next, idx_next, pl.BlockSpec((tc, d), lambda c: (c, 0)),
                  pl.BlockSpec((tc, ROUTER_LANES), lambda c: (c, 0)),
                  pl.BlockSpec((1, d), lambda c: (0, 0)), ANY_SPEC],
        out_specs=[pl.BlockSpec((tc, d), lambda c: (jnp.minimum(c, np_tiles - 1), 0)),
                   pl.BlockSpec((tc, d), lambda c: (jnp.maximum(c - np_tiles, 0), 0))],
        out_shape=[jax.ShapeDtypeStruct((n_p, d), F32), jax.ShapeDtypeStruct((n - n_p, d), F32)],
        scratch_shapes=[pltpu.VMEM((2, tc, d), F32), pltpu.VMEM((2, tc, d), F32),
                        pltpu.SemaphoreType.DMA((2,))],
        compiler_params=_cparams("arbitrary"),
        name="moe_combine_norm",
    )(d1, d2, d1, d2, x, route, g.reshape(1, d), yg)


def moe_plan(route):
    n = route.shape[0]
    sub = MOE_TILE // MOE_SUBS
    i1 = route[:, 0].astype(jnp.int32)
    i2 = route[:, 1].astype(jnp.int32)
    experts = jnp.arange(N_EXPERTS, dtype=jnp.int32)
    onehot = ((i1[:, None] == experts) | (i2[:, None] == experts)).astype(jnp.int32)
    counts = jnp.sum(onehot, axis=0)
    subs = (counts + sub - 1) // sub
    tiles = (counts + MOE_TILE - 1) // MOE_TILE
    tile_end = jnp.cumsum(tiles)
    tile_start = tile_end - tiles
    pos = jnp.cumsum(onehot, axis=0) - onehot
    dest = (tile_start * MOE_TILE)[None, :] + pos
    d1 = jnp.sum(jnp.where(i1[:, None] == experts, dest, 0), axis=1)
    d2 = jnp.sum(jnp.where(i2[:, None] == experts, dest, 0), axis=1)
    n_tiles = -(-(2 * n + N_EXPERTS * (MOE_TILE - 1)) // MOE_TILE)
    r = jnp.arange(n_tiles, dtype=jnp.int32)
    owner = jnp.minimum(jnp.sum((r[:, None] >= tile_end[None, :]).astype(jnp.int32), axis=1),
                        N_EXPERTS - 1)
    is_owner = owner[:, None] == experts[None, :]
    first_tile = jnp.sum(jnp.where(is_owner, tile_start[None, :], 0), axis=1)
    owner_subs = jnp.sum(jnp.where(is_owner, subs[None, :], 0), axis=1)
    n_used = tile_end[-1]
    n_sub = jnp.where(r < n_used, jnp.clip(owner_subs - (r - first_tile) * MOE_SUBS, 0, MOE_SUBS), 0)
    last_expert = jnp.sum(jnp.where(r == n_used - 1, owner, 0))
    tile_expert = jnp.where(r < n_used, owner, last_expert).astype(jnp.int32)
    tokens = jnp.arange(n, dtype=jnp.int32)
    row_token = jnp.zeros((n_tiles * MOE_TILE,), jnp.int32).at[jnp.concatenate([d1, d2])].set(
        jnp.concatenate([tokens, tokens]), unique_indices=True)
    return d1, d2, row_token, tile_expert, n_sub.astype(jnp.int32)


def _hgrn_kernel(p_ref, lb_ref, hn_ref, s0_ref, out_hbm, oa_ref, sout_ref,
                 st_ref, q_s, k_s, v_s, b_s, o_s, *, tt):
    del out_hbm
    c_len = HGRN_CHUNK
    t = pl.program_id(1)

    @pl.when(t == 0)
    def _():
        for h in range(N_HEADS):
            st_ref[h] = s0_ref[0, h].T

    lb = lb_ref[...]
    keep = 1.0 - lb
    z = p_ref[:, D_BRANCH:2 * D_BRANCH]
    e = jnp.exp(-jnp.abs(z))
    r = 1.0 / (1.0 + e)
    er = e * r
    logf = jnp.log(jnp.maximum(lb, LB_FLOOR) + keep * jnp.where(z >= 0.0, r, er))
    k_s[...] = keep * jnp.where(z >= 0.0, er, r)
    q_s[...] = _silu(p_ref[:, 0:D_BRANCH])
    v_s[...] = p_ref[:, 2 * D_BRANCH:3 * D_BRANCH]
    row = lax.broadcasted_iota(jnp.int32, (tt, tt), 0)
    col = lax.broadcasted_iota(jnp.int32, (tt, tt), 1)
    tri = jnp.where(((row ^ col) < c_len) & (col <= row), 1.0, 0.0).astype(BF16)
    lf_hi, lf_lo = _split_bf16(logf * LOG2_E)
    b_s[...] = _dot(tri, lf_hi) + _dot(tri, lf_lo)

    ones = jnp.ones((D_HEAD, D_HEAD), BF16)
    row_c = lax.broadcasted_iota(jnp.int32, (c_len, 1), 0)

    def chunk(c, carry):
        r0 = pl.multiple_of(c * c_len, c_len)
        qc = q_s[pl.ds(r0, c_len), :]
        kc = k_s[pl.ds(r0, c_len), :]
        vc = v_s[pl.ds(r0, c_len), :]
        bc = b_s[pl.ds(r0, c_len), :]
        b_last = bc[c_len - 1:c_len, :]
        qt = (qc * jnp.exp2(bc)).astype(BF16)
        kt = (kc * jnp.exp2(b_last - bc)).astype(BF16)
        decay = jnp.exp2(b_last)
        vb = vc.astype(BF16)
        xs = []
        for s in range(c_len):
            diff = jnp.where(row_c >= s, bc - bc[s:s + 1, :], NEG_BIG)
            xs.append((qc * kc[s:s + 1, :] * jnp.exp2(diff)).astype(BF16))
        x_all = jnp.concatenate(xs, axis=0)
        heads = [slice(h * D_HEAD, (h + 1) * D_HEAD) for h in range(N_HEADS)]
        states = [st_ref[h] for h in range(N_HEADS)]
        carried = [_dot_nt(qt[:, hs], st.astype(BF16)) for hs, st in zip(heads, states)]
        updates = [_dot_tn(vb[:, hs], kt[:, hs]) for hs in heads]
        sums = [_dot(x_all[:, hs], ones) for hs in heads]
        outs = []
        for hs, acc, y in zip(heads, carried, sums):
            for s in range(c_len):
                acc = acc + y[s * c_len:(s + 1) * c_len, :] * vc[s:s + 1, hs]
            outs.append(acc)
        for h, (hs, st, up) in enumerate(zip(heads, states, updates)):
            st_ref[h] = st * decay[:, hs] + up
        o_s[pl.ds(r0, c_len), :] = jnp.concatenate(outs, axis=1)
        return carry

    lax.fori_loop(0, tt // c_len, chunk, 0, unroll=HGRN_UNROLL)

    gate = _silu(p_ref[:, 3 * D_BRANCH:4 * D_BRANCH])
    hn = hn_ref[...]
    for h in range(N_HEADS):
        hs = slice(h * D_HEAD, (h + 1) * D_HEAD)
        oh = o_s[:, hs]
        oh = oh * lax.rsqrt(jnp.mean(oh * oh, axis=-1, keepdims=True) + EPS) * hn
        oa_ref[:, hs] = (oh * gate[:, hs]).astype(oa_ref.dtype)

    @pl.when(t == pl.num_programs(1) - 1)
    def _():
        for h in range(N_HEADS):
            sout_ref[0, h] = st_ref[h].T


def branch_buffer(n_rows):
    return jnp.zeros((n_rows, D_BRANCH), BF16)


def hgrn_mixer(proj, lb, hnorm, s0, row0, n_seq, seq_len, tt, out):
    nt = seq_len // tt
    rb0 = row0 // tt
    scr = [pltpu.VMEM((N_HEADS, D_HEAD, D_HEAD), F32)] + [pltpu.VMEM((tt, D_BRANCH), F32)] * 5
    return pl.pallas_call(
        functools.partial(_hgrn_kernel, tt=tt),
        grid=(n_seq, nt),
        in_specs=[pl.BlockSpec((tt, 4 * D_BRANCH), lambda s, t: (rb0 + s * nt + t, 0)),
                  pl.BlockSpec((1, D_BRANCH), lambda s, t: (0, 0)),
                  pl.BlockSpec((1, D_HEAD), lambda s, t: (0, 0)),
                  pl.BlockSpec((1, N_HEADS, D_HEAD, D_HEAD), lambda s, t: (s, 0, 0, 0)), ANY_SPEC],
        out_specs=[pl.BlockSpec((tt, D_BRANCH), lambda s, t: (rb0 + s * nt + t, 0)),
                   pl.BlockSpec((1, N_HEADS, D_HEAD, D_HEAD), lambda s, t: (s, 0, 0, 0))],
        out_shape=[jax.ShapeDtypeStruct(out.shape, out.dtype),
                   jax.ShapeDtypeStruct((n_seq, N_HEADS, D_HEAD, D_HEAD), F32)],
        scratch_shapes=scr,
        input_output_aliases={4: 0},
        compiler_params=_cparams("arbitrary", "arbitrary"),
        name="hgrn_mixer",
    )(proj, lb.reshape(1, D_BRANCH), hnorm.reshape(1, D_HEAD), s0, out)


def _pool_conv_kernel(p_ref, hp_ref, hc_ref, wp_ref, ps_ref, cw_ref, ob_hbm, oc_hbm,
                      ob_ref, oc_ref, pn_ref, cn_ref, xe, ue, *, tt, pos0):
    del ob_hbm, oc_hbm
    t = pl.program_id(1)

    @pl.when(t == 0)
    def _():
        xe[0:POOL_HALO, :] = hp_ref[0]
        ue[0:CONV_HALO, :] = hc_ref[0]

    @pl.when(t > 0)
    def _():
        xe[0:POOL_HALO, :] = xe[tt:tt + POOL_HALO, :]
        ue[0:CONV_HALO, :] = ue[tt:tt + CONV_HALO, :]

    x = p_ref[:, 0:D_BRANCH]
    xe[POOL_HALO:POOL_HALO + tt, :] = x
    u = p_ref[:, 3 * D_BRANCH:4 * D_BRANCH] * p_ref[:, D_BRANCH:2 * D_BRANCH]
    ue[CONV_HALO:CONV_HALO + tt, :] = u

    pos = pos0 + t * tt + lax.broadcasted_iota(jnp.int32, (tt, 1), 0)
    group = D_BRANCH // len(POOL_WINDOWS)
    for g, w in enumerate(POOL_WINDOWS):
        ls = slice(g * group, (g + 1) * group)
        s = x[:, ls]
        for j in range(1, w):
            s = s + xe[POOL_HALO - j:POOL_HALO - j + tt, ls]
        cnt = jnp.minimum(pos + 1, w).astype(F32)
        dlt = s / cnt - x[:, ls]
        y = _dot(dlt.astype(BF16), wp_ref[g].astype(BF16))
        ob_ref[:, ls] = (y * ps_ref[:, ls]).astype(ob_ref.dtype)

    y = ue[CONV_HALO - 2:CONV_HALO - 2 + tt, :] * cw_ref[0:1, :]
    for j in range(1, CONV_W):
        y = y + ue[CONV_HALO - 2 + j:CONV_HALO - 2 + j + tt, :] * cw_ref[j:j + 1, :]
    oc_ref[...] = (p_ref[:, 2 * D_BRANCH:3 * D_BRANCH] * y).astype(oc_ref.dtype)

    pn_ref[0] = xe[tt:tt + POOL_HALO, :]
    cn_ref[0] = ue[tt:tt + CONV_HALO, :]


def pool_conv_mixer(proj, hist_pool, hist_conv, w_pool, pool_scale, conv_w,
                    row0, n_seq, seq_len, tt, pos0, outs):
    nt = seq_len // tt
    rb0 = row0 // tt
    out_shape = jax.ShapeDtypeStruct(outs[0].shape, outs[0].dtype)
    cw = jnp.zeros((8, D_BRANCH), F32).at[:CONV_W].set(conv_w)
    out_row = pl.BlockSpec((tt, D_BRANCH), lambda s, t: (rb0 + s * nt + t, 0))
    return pl.pallas_call(
        functools.partial(_pool_conv_kernel, tt=tt, pos0=pos0),
        grid=(n_seq, nt),
        in_specs=[pl.BlockSpec((tt, 4 * D_BRANCH), lambda s, t: (rb0 + s * nt + t, 1)),
                  pl.BlockSpec((1, POOL_HALO, D_BRANCH), lambda s, t: (s, 0, 0)),
                  pl.BlockSpec((1, CONV_HALO, D_BRANCH), lambda s, t: (s, 0, 0)),
                  pl.BlockSpec(w_pool.shape, lambda s, t: (0, 0, 0)),
                  pl.BlockSpec((1, D_BRANCH), lambda s, t: (0, 0)),
                  pl.BlockSpec((8, D_BRANCH), lambda s, t: (0, 0)), ANY_SPEC, ANY_SPEC],
        out_specs=[out_row, out_row,
                   pl.BlockSpec((1, POOL_HALO, D_BRANCH), lambda s, t: (s, 0, 0)),
                   pl.BlockSpec((1, CONV_HALO, D_BRANCH), lambda s, t: (s, 0, 0))],
        out_shape=[out_shape, out_shape,
                   jax.ShapeDtypeStruct((n_seq, POOL_HALO, D_BRANCH), F32),
                   jax.ShapeDtypeStruct((n_seq, CONV_HALO, D_BRANCH), F32)],
        scratch_shapes=[pltpu.VMEM((tt + POOL_HALO, D_BRANCH), F32),
                        pltpu.VMEM((tt + CONV_HALO, D_BRANCH), F32)],
        input_output_aliases={6: 0, 7: 1},
        compiler_params=_cparams("arbitrary", "arbitrary"),
        name="pool_conv_mixer",
    )(proj, hist_pool, hist_conv, w_pool, pool_scale.reshape(1, D_BRANCH), cw, *outs)


def _sb_tile(q_ref, kv, o_acc, run, masked):
    tq = q_ref.shape[0]
    scale = D_HEAD ** -0.5
    kr = lax.broadcasted_iota(jnp.int32, (SB_TILE, 2 * SB_TILE), 0)
    kc = lax.broadcasted_iota(jnp.int32, (SB_TILE, 2 * SB_TILE), 1)
    sums = jnp.where((kc >= SB_TILE) | (kr > kc), 1.0, 0.0).astype(BF16)
    if masked:
        qi = lax.broadcasted_iota(jnp.int32, (tq, SB_TILE), 0)
        ki = lax.broadcasted_iota(jnp.int32, (tq, SB_TILE), 1)
        mask = ki < qi
    run_old = run[...]
    heads = [slice(h * D_HEAD, (h + 1) * D_HEAD) for h in range(N_HEADS)]
    kvs = [kv(h) for h in range(N_HEADS)]
    zs = [_dot_nt(q_ref[:, hs], k.astype(BF16)) * scale for hs, (k, _) in zip(heads, kvs)]
    lks = []
    for z in zs:
        log_keep = -(jnp.maximum(z, 0.0) + jnp.log1p(jnp.exp(-jnp.abs(z))))
        lks.append(jnp.where(mask, log_keep, 0.0) if masked else log_keep)
    splits = [_split_bf16(lk) for lk in lks]
    sms = [_dot(hi, sums) + _dot(lo, sums) for hi, lo in splits]
    probs = []
    for hs, z, lk, sm in zip(heads, zs, lks, sms):
        a = jnp.exp(z + lk + sm[:, 0:SB_TILE] + run_old[:, hs])
        probs.append((jnp.where(mask, a, 0.0) if masked else a).astype(BF16))
    o_new = [_dot(a, v.astype(BF16)) for a, (_, v) in zip(probs, kvs)]
    o_acc[...] += jnp.concatenate(o_new, axis=1)
    run[...] = run_old + jnp.concatenate([sm[:, SB_TILE:2 * SB_TILE] for sm in sms], axis=1)


def _sb_alive(run):
    return (jnp.max(run[...]) > SB_DONE).astype(jnp.int32)


def _sb_past_loop(q_ref, kv_tile, o_acc, run, n_tiles):
    def cond(c):
        j, alive = c
        return (j >= 0) & (alive > 0)

    def body(c):
        j, _ = c
        r0 = pl.multiple_of(j * SB_TILE, SB_TILE)
        _sb_tile(q_ref, functools.partial(kv_tile, r0), o_acc, run, masked=False)
        return j - 1, _sb_alive(run)

    lax.while_loop(cond, body, (jnp.int32(n_tiles) - 1, _sb_alive(run)))


def _sb_prompt_kernel(q_ref, k_ref, v_ref, out_hbm, o_ref, o_acc, run):
    del out_hbm
    i = pl.program_id(1)
    o_acc[...] = jnp.zeros_like(o_acc)
    run[...] = jnp.zeros_like(run)

    def kv_tile(r0, h):
        hs = slice(h * D_HEAD, (h + 1) * D_HEAD)
        return k_ref[pl.ds(r0, SB_TILE), hs], v_ref[pl.ds(r0, SB_TILE), hs]

    _sb_tile(q_ref, functools.partial(kv_tile, pl.multiple_of(i * SB_TILE, SB_TILE)), o_acc, run,
             masked=True)
    _sb_past_loop(q_ref, kv_tile, o_acc, run, i)
    o_ref[...] = o_acc[...].astype(o_ref.dtype)


def sb_prompt_mixer(qkv, n_seq, seq_len, out):
    nq = seq_len // SB_TILE
    seq = lambda c: pl.BlockSpec((seq_len, D_BRANCH), lambda s, i: (s, c))
    return pl.pallas_call(
        _sb_prompt_kernel,
        grid=(n_seq, nq),
        in_specs=[pl.BlockSpec((SB_TILE, D_BRANCH), lambda s, i: (s * nq + i, 0)), seq(1), seq(2),
                  ANY_SPEC],
        out_specs=pl.BlockSpec((SB_TILE, D_BRANCH), lambda s, i: (s * nq + i, 0)),
        out_shape=jax.ShapeDtypeStruct(out.shape, out.dtype),
        scratch_shapes=[pltpu.VMEM((SB_TILE, D_BRANCH), F32), pltpu.VMEM((SB_TILE, D_BRANCH), F32)],
        input_output_aliases={3: 0},
        compiler_params=_cparams("arbitrary", "arbitrary"),
        name="sb_prompt_mixer",
    )(qkv, qkv, qkv, out)


def _sb_sample_kernel(qkv_ref, kp_ref, vp_ref, out_hbm, o_ref, o_acc, run, *, n_past_tiles):
    del out_hbm
    o_acc[...] = jnp.zeros_like(o_acc)
    run[...] = jnp.zeros_like(run)
    tq = qkv_ref.shape[0]
    q_ref = qkv_ref.at[:, 0:D_BRANCH]
    pad = jnp.zeros((SB_TILE - tq, D_BRANCH), qkv_ref.dtype)
    k_new = jnp.concatenate([qkv_ref[:, D_BRANCH:2 * D_BRANCH], pad], axis=0)
    v_new = jnp.concatenate([qkv_ref[:, 2 * D_BRANCH:3 * D_BRANCH], pad], axis=0)

    def kv_new(h):
        hs = slice(h * D_HEAD, (h + 1) * D_HEAD)
        return k_new[:, hs], v_new[:, hs]

    def kv_past(r0, h):
        return kp_ref[pl.ds(r0, SB_TILE), h, :], vp_ref[pl.ds(r0, SB_TILE), h, :]

    _sb_tile(q_ref, kv_new, o_acc, run, masked=True)
    _sb_past_loop(q_ref, kv_past, o_acc, run, n_past_tiles)
    o_ref[...] = o_acc[...].astype(o_ref.dtype)


def sb_sample_mixer(qkv, k_cache, v_cache, layer, row0, n_seq, seq_len, out):
    past = k_cache.shape[2]
    rb0 = row0 // seq_len
    cache = pl.BlockSpec((None, None, past, N_HEADS, D_HEAD), lambda s: (layer, s, 0, 0, 0))
    return pl.pallas_call(
        functools.partial(_sb_sample_kernel, n_past_tiles=past // SB_TILE),
        grid=(n_seq,),
        in_specs=[pl.BlockSpec((seq_len, 3 * D_BRANCH), lambda s: (rb0 + s, 0)), cache, cache, ANY_SPEC],
        out_specs=pl.BlockSpec((seq_len, D_BRANCH), lambda s: (rb0 + s, 0)),
        out_shape=jax.ShapeDtypeStruct(out.shape, out.dtype),
        scratch_shapes=[pltpu.VMEM((seq_len, D_BRANCH), F32), pltpu.VMEM((seq_len, D_BRANCH), F32)],
        input_output_aliases={3: 0},
        compiler_params=_cparams("arbitrary"),
        name="sb_sample_mixer",
    )(qkv, k_cache, v_cache, out)


def _tile(n, want):
    if n <= want:
        return n
    for t in range(want, 7, -8):
        if n % t == 0:
            return t
    return n


def kernel(x_prompt, x_sample, state_hgrn, state_pool, state_conv, cache_k, cache_v, norm_mix, norm_ffn, norm_final, w_in, hgrn_lower_bound, hgrn_out_norm, pool_w, pool_scale, conv_w, w_branch, w_gate, b_gate, w_out, ffn_w1, ffn_w3, ffn_w2, moe_router, moe_w1, moe_w3, moe_w2):
    depth = w_in.shape[0]
    pb, pt, d = x_prompt.shape
    sb, st, _ = x_sample.shape
    past = cache_k.shape[2]
    n_p = pb * pt
    n_s = sb * st
    n = n_p + n_s
    assert depth % 2 == 0, "the last layer is expected to be a MoE layer"
    x = (x_prompt.reshape(n_p, d), x_sample.reshape(n_s, d))

    tm_row = _tile(n_s, 512)
    tm_mm = _tile(n_s, 2048)
    tm_half = tm_mm // 2
    tc = _tile(n_s, 256)
    tt_p = _tile(pt, 256)
    assert n_p % tm_mm == 0 and n_p % tc == 0 and n_p % tm_row == 0

    sm = jax.nn.softmax(hgrn_lower_bound.astype(F32), axis=0)
    lbs = jnp.cumsum(sm, axis=0) - sm[0:1]
    ffn_w2_bf16 = ffn_w2.astype(BF16)

    zeros_state = jnp.zeros((pb, N_HEADS, D_HEAD, D_HEAD), F32)
    zeros_pool = jnp.zeros((pb, POOL_HALO, D_BRANCH), F32)
    zeros_conv = jnp.zeros((pb, CONV_HALO, D_BRANCH), F32)
    pool_pad = ((0, 0), (0, 0), (POOL_HALO - state_pool.shape[2], 0), (0, 0))
    conv_pad = ((0, 0), (0, 0), (CONV_HALO - state_conv.shape[2], 0), (0, 0))
    sample_pool = jnp.pad(state_pool, pool_pad)
    sample_conv = jnp.pad(state_conv, conv_pad)

    outs = {k: [] for k in ("p_h", "p_pool", "p_conv", "s_h", "s_pool", "s_conv")}
    caches = cache_buffers(depth, n_p, n_s)
    xn = rmsnorm(x, norm_mix[0], BF16, tm_row)
    y_p = y_s = None
    for l in range(depth):
        proj = in_proj(xn, w_in, l, 8 * D_BRANCH, tm_mm, D_BRANCH)
        qkv16, *caches = qkv_proj(xn, w_in, l, 8 * D_BRANCH, n_p, caches, tm_half)

        oa, hs_p = hgrn_mixer(proj, lbs[l], hgrn_out_norm[l], zeros_state, 0, pb, pt, tt_p,
                              branch_buffer(n))
        oa, hs_s = hgrn_mixer(proj, lbs[l], hgrn_out_norm[l], state_hgrn[l], n_p, sb, st, st, oa)
        ob, oc, pool_p, conv_p = pool_conv_mixer(
            proj, zeros_pool, zeros_conv, pool_w[l], pool_scale[l], conv_w[l], 0, pb, pt, tt_p, 0,
            (branch_buffer(n), branch_buffer(n)))
        ob, oc, pool_s, conv_s = pool_conv_mixer(
            proj, sample_pool[l], sample_conv[l], pool_w[l], pool_scale[l], conv_w[l],
            n_p, sb, st, st, past, (ob, oc))
        od = sb_prompt_mixer(qkv16, pb, pt, branch_buffer(n))
        od = sb_sample_mixer(qkv16, cache_k, cache_v, l, n_p, sb, st, od)

        merged = gate_merge(xn, (oa, ob, oc, od), w_gate, b_gate, w_branch, l, tm_half, 256)
        if l % 2 == 0:
            x, h = mix_out(merged, w_out[l].astype(BF16), x, norm_ffn[l], tm_row)
        else:
            x = out_proj(merged, w_out, l, x, tm_mm, D_BRANCH)

        outs["p_h"].append(hs_p)
        outs["s_h"].append(hs_s)
        outs["p_pool"].append(pool_p[:, 1:])
        outs["s_pool"].append(pool_s[:, 1:])
        outs["p_conv"].append(conv_p[:, CONV_HALO - (CONV_W - 1):])
        outs["s_conv"].append(conv_s[:, CONV_HALO - (CONV_W - 1):])

        if l % 2 == 0:
            act = dense_glu(h, ffn_w1, ffn_w3, l // 2, tm_mm, D_BRANCH)
            x = out_proj(act, ffn_w2_bf16, l // 2, x, tm_half, D_BRANCH)
            xn = rmsnorm(x, norm_mix[l + 1], BF16, tm_row)
        else:
            h32, route = rmsnorm_router(x, norm_ffn[l], moe_router[l // 2], tm_row)
            d1, d2, row_token, tile_expert, n_sub = moe_plan(route)
            act = grouped_glu(h32, row_token, tile_expert, n_sub, moe_w1, moe_w3, l // 2,
                              D_BRANCH)
            yg = grouped_down(act, tile_expert, n_sub, moe_w2, l // 2, MOE_TILE // 2, 256)
            if l + 1 < depth:
                raise NotImplementedError("a MoE layer followed by another layer")
            y_p, y_s = moe_combine_norm(x, route, yg, d1, d2, norm_final, n_p, tc)

    pk, pv, sk, sv = caches
    st_ = lambda k: jnp.stack(outs[k])
    return (y_p.reshape(pb, pt, d), y_s.reshape(sb, st, d),
            st_("p_h"), st_("p_pool"), st_("p_conv"),
            pk.reshape(depth, pb, pt, N_HEADS, D_HEAD), pv.reshape(depth, pb, pt, N_HEADS, D_HEAD),
            st_("s_h"), st_("s_pool"), st_("s_conv"),
            sk.reshape(depth, sb, st, N_HEADS, D_HEAD), sv.reshape(depth, sb, st, N_HEADS, D_HEAD))
```

```python
import functools

import jax
import jax.numpy as jnp
from jax import lax
from jax.experimental import pallas as pl
from jax.experimental.pallas import tpu as pltpu

F32 = jnp.float32
BF16 = jnp.bfloat16

EPS = 1e-6
NEG_BIG = -1e30
LB_FLOOR = 1e-30
LOG2_E = 1.4426950408889634
N_HEADS = 4
D_HEAD = 128
D_BRANCH = N_HEADS * D_HEAD
POOL_WINDOWS = (2, 4, 8, 16)
POOL_HALO = 16
CONV_W = 3
CONV_HALO = 8
N_EXPERTS = 8
HGRN_CHUNK = 16
HGRN_UNROLL = 4
SB_TILE = 128
SB_DONE = -120.0
ROUTER_LANES = 128
MOE_TILE = 2048
MOE_SUBS = 4
DMA_UNROLL = 8

V7X_VMEM_LIMIT_BYTES = 56 * 1024 * 1024

ANY_SPEC = pl.BlockSpec(memory_space=pl.ANY)


def _cparams(*sem):
    return pltpu.CompilerParams(dimension_semantics=sem,
                                vmem_limit_bytes=V7X_VMEM_LIMIT_BYTES)


def _split_bf16(x):
    hi = x.astype(BF16)
    lo = (x - hi.astype(F32)).astype(BF16)
    return hi, lo


def _dot(a, b):
    return jnp.dot(a, b, preferred_element_type=F32)


def _dot_nt(a, b):
    return lax.dot_general(a, b, (((1,), (1,)), ((), ())), preferred_element_type=F32)


def _dot_tn(a, b):
    return lax.dot_general(a, b, (((0,), (0,)), ((), ())), preferred_element_type=F32)


def _silu(a):
    return a * jax.nn.sigmoid(a)


def _rmsnorm_rows(x, g):
    return x * lax.rsqrt(jnp.mean(x * x, axis=-1, keepdims=True) + EPS) * g


def _two_source_specs(xs, tm, width, col_of):
    def index_map(i, *j, t0, nt):
        inside = (i >= t0) & (i < t0 + nt)
        return jnp.clip(i - t0, 0, nt - 1), jnp.where(inside, col_of(*j), 0)

    starts, specs, t0 = [], [], 0
    for x in xs:
        nt = x.shape[0] // tm
        starts.append(t0)
        specs.append(pl.BlockSpec((tm, width), functools.partial(index_map, t0=t0, nt=nt)))
        t0 += nt
    return specs, starts, t0


def _pick_source(refs, starts):
    i = pl.program_id(0)
    x = refs[0][...]
    for ref, t0 in zip(refs[1:], starts[1:]):
        x = jnp.where(i >= t0, ref[...], x)
    return x


def _rms_kernel(*refs, starts):
    g_ref, o_ref = refs[-2:]
    x = _pick_source(refs[:-2], starts)
    o_ref[...] = _rmsnorm_rows(x, g_ref[...]).astype(o_ref.dtype)


def rmsnorm(xs, g, out_dtype, tm):
    xs = xs if isinstance(xs, tuple) else (xs,)
    d = xs[0].shape[1]
    specs, starts, n_tiles = _two_source_specs(xs, tm, d, lambda: 0)
    return pl.pallas_call(
        functools.partial(_rms_kernel, starts=starts),
        grid=(n_tiles,),
        in_specs=specs + [pl.BlockSpec((1, d), lambda i: (0, 0))],
        out_specs=pl.BlockSpec((tm, d), lambda i: (i, 0)),
        out_shape=jax.ShapeDtypeStruct((n_tiles * tm, d), out_dtype),
        compiler_params=_cparams("arbitrary"),
        name="rmsnorm",
    )(*xs, g.reshape(1, d))


def _route(h, r_ref, h32_ref, route_ref):
    h_hi, h_lo = _split_bf16(h)
    h32_ref[...] = h_hi.astype(F32)
    r_hi, r_lo = _split_bf16(r_ref[...])
    logits = _dot(h_hi, r_hi) + (_dot(h_lo, r_hi) + _dot(h_hi, r_lo))
    lane = lax.broadcasted_iota(jnp.int32, logits.shape, 1).astype(F32)
    neg_inf = jnp.float32(-jnp.inf)
    l1 = jnp.where(lane < N_EXPERTS, logits, neg_inf)
    m1 = jnp.max(l1, axis=-1, keepdims=True)
    i1 = jnp.min(jnp.where(l1 == m1, lane, float(ROUTER_LANES)), axis=-1, keepdims=True)
    l2 = jnp.where(lane == i1, neg_inf, l1)
    m2 = jnp.max(l2, axis=-1, keepdims=True)
    i2 = jnp.min(jnp.where(l2 == m2, lane, float(ROUTER_LANES)), axis=-1, keepdims=True)
    e = jnp.exp(m2 - m1)
    w1 = 1.0 / (1.0 + e)
    w2 = e / (1.0 + e)
    route_ref[...] = jnp.where(lane == 0.0, i1, jnp.where(lane == 1.0, i2, jnp.where(
        lane == 2.0, w1, jnp.where(lane == 3.0, w2, 0.0))))


def _rms_router_kernel(x_ref, g_ref, r_ref, h32_ref, route_ref):
    _route(_rmsnorm_rows(x_ref[...], g_ref[...]), r_ref, h32_ref, route_ref)


def rmsnorm_router(x, g, router, tm):
    n, d = x.shape
    r_pad = jnp.zeros((d, ROUTER_LANES), F32).at[:, :N_EXPERTS].set(router)
    row = pl.BlockSpec((tm, d), lambda i: (i, 0))
    return pl.pallas_call(
        _rms_router_kernel,
        grid=(n // tm,),
        in_specs=[row, pl.BlockSpec((1, d), lambda i: (0, 0)),
                  pl.BlockSpec((d, ROUTER_LANES), lambda i: (0, 0))],
        out_specs=[row, pl.BlockSpec((tm, ROUTER_LANES), lambda i: (i, 0))],
        out_shape=[jax.ShapeDtypeStruct((n, d), F32), jax.ShapeDtypeStruct((n, ROUTER_LANES), F32)],
        compiler_params=_cparams("parallel"),
        name="rmsnorm_router",
    )(x, g.reshape(1, d), r_pad)


def _mix_out_kernel(m_ref, w_ref, *refs, starts):
    g_ref, xo_ref, h_ref = refs[-3:]
    x = _pick_source(refs[:-3], starts) + _dot(m_ref[...], w_ref[...])
    xo_ref[...] = x
    h_ref[...] = _rmsnorm_rows(x, g_ref[...]).astype(h_ref.dtype)


def mix_out(merged, w, xs, g, tm):
    xs = xs if isinstance(xs, tuple) else (xs,)
    n, k = merged.shape
    d = w.shape[1]
    specs, starts, n_tiles = _two_source_specs(xs, tm, d, lambda: 0)
    assert n_tiles * tm == n
    row = pl.BlockSpec((tm, d), lambda i: (i, 0))
    return pl.pallas_call(
        functools.partial(_mix_out_kernel, starts=starts),
        grid=(n_tiles,),
        in_specs=[pl.BlockSpec((tm, k), lambda i: (i, 0)), pl.BlockSpec((k, d), lambda i: (0, 0))]
                 + specs + [pl.BlockSpec((1, d), lambda i: (0, 0))],
        out_specs=[row, row],
        out_shape=[jax.ShapeDtypeStruct((n, d), F32), jax.ShapeDtypeStruct((n, d), BF16)],
        compiler_params=_cparams("arbitrary"),
        name="mix_out",
    )(merged, w, *xs, g.reshape(1, d))


def _proj_kernel(x_ref, w_ref, o_ref):
    o_ref[...] = _dot(x_ref[...], w_ref[...].astype(BF16)).astype(o_ref.dtype)


def in_proj(xn, w_all, layer, n_cols, tm, tn):
    n, k = xn.shape
    return pl.pallas_call(
        _proj_kernel,
        grid=(n // tm, n_cols // tn),
        in_specs=[pl.BlockSpec((tm, k), lambda i, j: (i, 0)),
                  pl.BlockSpec((None, k, tn), lambda i, j: (layer, 0, j))],
        out_specs=pl.BlockSpec((tm, tn), lambda i, j: (i, j)),
        out_shape=jax.ShapeDtypeStruct((n, n_cols), F32),
        compiler_params=_cparams("parallel", "arbitrary"),
        name="in_proj",
    )(xn, w_all)


def _qkv_kernel(x_ref, w_ref, pk_hbm, pv_hbm, sk_hbm, sv_hbm,
                o16_ref, pk_ref, pv_ref, sk_ref, sv_ref, *, np_tiles):
    del pk_hbm, pv_hbm, sk_hbm, sv_hbm
    r = _dot(x_ref[...], w_ref[...].astype(BF16))
    o16_ref[...] = r.astype(o16_ref.dtype)
    is_prompt = pl.program_id(0) < np_tiles
    j = pl.program_id(1)

    def put(dst):
        for h in range(N_HEADS):
            dst[:, h, :] = r[:, h * D_HEAD:(h + 1) * D_HEAD]

    for col, p_ref, s_ref in ((1, pk_ref, sk_ref), (2, pv_ref, sv_ref)):
        pl.when((j == col) & is_prompt)(functools.partial(put, p_ref))
        pl.when((j == col) & jnp.logical_not(is_prompt))(functools.partial(put, s_ref))


def cache_buffers(depth, n_p, n_s):
    return [jnp.zeros((depth * n, N_HEADS, D_HEAD), F32) for n in (n_p, n_p, n_s, n_s)]


def qkv_proj(xn, w_all, layer, col_start, n_p, caches, tm):
    n, k = xn.shape
    n_s = n - n_p
    np_tiles, ns_tiles = n_p // tm, n_s // tm
    cb0 = col_start // D_BRANCH
    head_blk = (tm, N_HEADS, D_HEAD)
    p_spec = pl.BlockSpec(head_blk, lambda i, j: (layer * np_tiles + jnp.minimum(i, np_tiles - 1), 0, 0))
    s_spec = pl.BlockSpec(head_blk, lambda i, j: (layer * ns_tiles + jnp.maximum(i - np_tiles, 0), 0, 0))
    return pl.pallas_call(
        functools.partial(_qkv_kernel, np_tiles=np_tiles),
        grid=(n // tm, 3),
        in_specs=[pl.BlockSpec((tm, k), lambda i, j: (i, 0)),
                  pl.BlockSpec((None, k, D_BRANCH), lambda i, j: (layer, 0, cb0 + j))]
                 + [ANY_SPEC] * 4,
        out_specs=[pl.BlockSpec((tm, D_BRANCH), lambda i, j: (i, j)), p_spec, p_spec, s_spec, s_spec],
        out_shape=[jax.ShapeDtypeStruct((n, 3 * D_BRANCH), BF16)]
                  + [jax.ShapeDtypeStruct(c.shape, c.dtype) for c in caches],
        input_output_aliases={2: 1, 3: 2, 4: 3, 5: 4},
        compiler_params=_cparams("arbitrary", "arbitrary"),
        name="qkv_proj",
    )(xn, w_all, *caches)


def _out_proj_kernel(m_ref, w_ref, *refs, starts):
    o_ref = refs[-1]
    o_ref[...] = _pick_source(refs[:-1], starts) + _dot(m_ref[...], w_ref[...].astype(BF16))


def out_proj(a, w_all, layer, xs, tm, tn):
    xs = xs if isinstance(xs, tuple) else (xs,)
    n, k = a.shape
    d = xs[0].shape[1]
    specs, starts, n_tiles = _two_source_specs(xs, tm, tn, lambda j: j)
    assert n_tiles * tm == n
    return pl.pallas_call(
        functools.partial(_out_proj_kernel, starts=starts),
        grid=(n_tiles, d // tn),
        in_specs=[pl.BlockSpec((tm, k), lambda i, j: (i, 0)),
                  pl.BlockSpec((None, k, tn), lambda i, j: (layer, 0, j))] + specs,
        out_specs=pl.BlockSpec((tm, tn), lambda i, j: (i, j)),
        out_shape=jax.ShapeDtypeStruct((n, d), F32),
        compiler_params=_cparams("arbitrary", "arbitrary"),
        name="out_proj",
    )(a, w_all, *xs)


def _gate_merge_kernel(xn_ref, oa_ref, ob_ref, oc_ref, od_ref,
                       wg0_ref, wg1_ref, wg2_ref, wg3_ref,
                       bg0_ref, bg1_ref, bg2_ref, bg3_ref, wb_ref, o_ref):
    xn = xn_ref[...]
    branches = (oa_ref, ob_ref, oc_ref, od_ref)
    gates = (wg0_ref, wg1_ref, wg2_ref, wg3_ref)
    biases = (bg0_ref, bg1_ref, bg2_ref, bg3_ref)
    acc = None
    for n in range(4):
        g = jax.nn.sigmoid(_dot(xn, gates[n][...].astype(BF16)) + biases[n][...])
        p = _dot(branches[n][...], wb_ref[n].astype(BF16))
        acc = g * p if acc is None else acc + g * p
    o_ref[...] = acc.astype(o_ref.dtype)


def gate_merge(xn, branches, w_gate, b_gate, w_branch, layer, tm, tn):
    n, d = xn.shape
    nb = d // tn
    db = branches[0].shape[1]
    b3 = b_gate.reshape(b_gate.shape[0], 1, b_gate.shape[1])
    row = pl.BlockSpec((tm, d), lambda i, j: (i, 0))
    br = pl.BlockSpec((tm, db), lambda i, j: (i, 0))
    wg = [pl.BlockSpec((None, d, tn), functools.partial(lambda i, j, m: (layer, 0, m * nb + j), m=m))
          for m in range(4)]
    bg = [pl.BlockSpec((None, 1, tn), functools.partial(lambda i, j, m: (layer, 0, m * nb + j), m=m))
          for m in range(4)]
    wb = pl.BlockSpec((None, 4, db, tn), lambda i, j: (layer, 0, 0, j))
    return pl.pallas_call(
        _gate_merge_kernel,
        grid=(n // tm, nb),
        in_specs=[row, br, br, br, br] + wg + bg + [wb],
        out_specs=pl.BlockSpec((tm, tn), lambda i, j: (i, j)),
        out_shape=jax.ShapeDtypeStruct((n, d), BF16),
        compiler_params=_cparams("parallel", "arbitrary"),
        name="gate_merge",
    )(xn, *branches, w_gate, w_gate, w_gate, w_gate, b3, b3, b3, b3, w_branch)


def _glu_kernel(h_ref, w1_ref, w3_ref, o_ref):
    h = h_ref[...]
    a = _dot(h, w1_ref[...].astype(BF16))
    b = _dot(h, w3_ref[...].astype(BF16))
    o_ref[...] = (_silu(a) * b).astype(o_ref.dtype)


def dense_glu(h, w1, w3, idx, tm, tn):
    n, d = h.shape
    dff = w1.shape[-1]
    return pl.pallas_call(
        _glu_kernel,
        grid=(n // tm, dff // tn),
        in_specs=[pl.BlockSpec((tm, d), lambda i, j: (i, 0)),
                  pl.BlockSpec((None, d, tn), lambda i, j: (idx, 0, j)),
                  pl.BlockSpec((None, d, tn), lambda i, j: (idx, 0, j))],
        out_specs=pl.BlockSpec((tm, tn), lambda i, j: (i, j)),
        out_shape=jax.ShapeDtypeStruct((n, dff), BF16),
        compiler_params=_cparams("parallel", "arbitrary"),
        name="dense_glu",
    )(h, w1, w3)


def _row_dma_loop(trips, copies, start):
    def body(g, c):
        for u in range(DMA_UNROLL):
            for cp in copies(g * DMA_UNROLL + u):
                if start:
                    cp.start()
                else:
                    cp.wait()
        return c
    lax.fori_loop(0, trips, body, 0)


def _grouped_glu_kernel(te_ref, ns_ref, tok_ref, h_hbm, w1_ref, w3_ref, o_ref,
                        g_buf, hb_ref, sem, *, sub):
    del te_ref
    n_sub = ns_ref[pl.program_id(0)]
    g_rows = g_buf.shape[0]
    subs_per_pass = g_rows // sub

    for g in range(hb_ref.shape[0] // g_rows):
        live_here = jnp.clip(n_sub - g * subs_per_pass, 0, subs_per_pass)

        @pl.when((live_here > 0) & (pl.program_id(1) == 0))
        def _():
            def copies(i):
                return (pltpu.make_async_copy(h_hbm.at[pl.ds(tok_ref[0, 0, g * g_rows + i], 1), :],
                                              g_buf.at[pl.ds(i, 1), :], sem),)

            _row_dma_loop(live_here * (sub // DMA_UNROLL), copies, start=True)
            _row_dma_loop(live_here * (sub // DMA_UNROLL), copies, start=False)
            for m in range(subs_per_pass):
                @pl.when(m < live_here)
                def _():
                    hb_ref[pl.ds(g * g_rows + m * sub, sub), :] = g_buf[pl.ds(m * sub, sub), :].astype(BF16)

    n_subs = hb_ref.shape[0] // sub
    full = n_sub == n_subs

    @pl.when(full)
    def _():
        _glu_kernel(hb_ref, w1_ref, w3_ref, o_ref)

    for k in range(n_subs):
        @pl.when(n_sub == k)
        def _():
            if k:
                h = hb_ref[0:k * sub, :]
                a = _dot(h, w1_ref[...].astype(BF16))
                b = _dot(h, w3_ref[...].astype(BF16))
                o_ref[0:k * sub, :] = (_silu(a) * b).astype(o_ref.dtype)
            o_ref[k * sub:, :] = jnp.zeros(((n_subs - k) * sub, o_ref.shape[1]), o_ref.dtype)


def _expert_block(idx, n_blocks, tiles_per_entry):
    def index_map(r, j, te, ns):
        e = r // tiles_per_entry
        live = ns[e] > (r % tiles_per_entry) * (MOE_SUBS // tiles_per_entry)
        return (idx, te[e], 0, jnp.where(live, j, n_blocks - 1))
    return index_map


def grouped_glu(h32, row_token, tile_expert, n_sub, w1, w3, idx, tn):
    d = h32.shape[1]
    p = row_token.shape[0]
    nr = p // MOE_TILE
    dff = w1.shape[-1]
    nj = dff // tn
    w_spec = pl.BlockSpec((None, None, d, tn), _expert_block(idx, nj, 1))
    grid_spec = pltpu.PrefetchScalarGridSpec(
        num_scalar_prefetch=2,
        grid=(nr, nj),
        in_specs=[pl.BlockSpec((1, 1, MOE_TILE), lambda r, j, te, ns: (r, 0, 0),
                               memory_space=pltpu.SMEM),
                  ANY_SPEC, w_spec, w_spec],
        out_specs=pl.BlockSpec((MOE_TILE, tn), lambda r, j, te, ns: (r, j)),
        scratch_shapes=[pltpu.VMEM((MOE_TILE // 2, d), F32), pltpu.VMEM((MOE_TILE, d), BF16),
                        pltpu.SemaphoreType.DMA],
    )
    return pl.pallas_call(
        functools.partial(_grouped_glu_kernel, sub=MOE_TILE // MOE_SUBS),
        grid_spec=grid_spec,
        out_shape=jax.ShapeDtypeStruct((p, dff), BF16),
        compiler_params=_cparams("arbitrary", "arbitrary"),
        name="grouped_glu",
    )(tile_expert, n_sub, row_token.reshape(nr, 1, MOE_TILE), h32, w1, w3)


def _grouped_down_kernel(te_ref, ns_ref, a_ref, w2_ref, o_ref, *, tiles_per_entry):
    del te_ref
    r = pl.program_id(0)
    subs_here = MOE_SUBS // tiles_per_entry
    sub = o_ref.shape[0] // subs_here
    live = jnp.clip(ns_ref[r // tiles_per_entry] - (r % tiles_per_entry) * subs_here, 0, subs_here)

    @pl.when(live == subs_here)
    def _():
        o_ref[...] = _dot(a_ref[...], w2_ref[...].astype(BF16))

    @pl.when(live < subs_here)
    def _():
        w2 = w2_ref[...].astype(BF16)
        for m in range(subs_here):
            rows = pl.ds(m * sub, sub)

            @pl.when(m < live)
            def _():
                o_ref[rows, :] = _dot(a_ref[rows, :], w2)

            @pl.when(m >= live)
            def _():
                o_ref[rows, :] = jnp.zeros((sub, o_ref.shape[1]), o_ref.dtype)


def grouped_down(act, tile_expert, n_sub, w2, idx, tm, tn):
    p, dff = act.shape
    d = w2.shape[-1]
    nj = d // tn
    per = MOE_TILE // tm
    grid_spec = pltpu.PrefetchScalarGridSpec(
        num_scalar_prefetch=2,
        grid=(p // tm, nj),
        in_specs=[pl.BlockSpec((tm, dff), lambda r, j, te, ns: (r, 0)),
                  pl.BlockSpec((None, None, dff, tn), _expert_block(idx, nj, per))],
        out_specs=pl.BlockSpec((tm, tn), lambda r, j, te, ns: (r, j)),
    )
    return pl.pallas_call(
        functools.partial(_grouped_down_kernel, tiles_per_entry=per),
        grid_spec=grid_spec,
        out_shape=jax.ShapeDtypeStruct((p, d), F32),
        compiler_params=_cparams("arbitrary", "arbitrary"),
        name="grouped_down",
    )(tile_expert, n_sub, act, w2)


def _combine_kernel(d1_ref, d2_ref, d1n_ref, d2n_ref, x_ref, r_ref, g_ref, yg_hbm, yp_ref, ys_ref,
                    y1_buf, y2_buf, sem, *, tc, np_tiles):
    c = pl.program_id(0)
    slot = c % 2

    def copies(i1_ref, i2_ref, s):
        def of_token(t):
            return (pltpu.make_async_copy(yg_hbm.at[pl.ds(i1_ref[0, 0, t], 1), :],
                                          y1_buf.at[s, pl.ds(t, 1), :], sem.at[s]),
                    pltpu.make_async_copy(yg_hbm.at[pl.ds(i2_ref[0, 0, t], 1), :],
                                          y2_buf.at[s, pl.ds(t, 1), :], sem.at[s]))
        return of_token

    @pl.when(c == 0)
    def _():
        _row_dma_loop(tc // DMA_UNROLL, copies(d1_ref, d2_ref, slot), start=True)

    @pl.when(c + 1 < pl.num_programs(0))
    def _():
        _row_dma_loop(tc // DMA_UNROLL, copies(d1n_ref, d2n_ref, 1 - slot), start=True)

    _row_dma_loop(tc // DMA_UNROLL, copies(d1_ref, d2_ref, slot), start=False)
    route = r_ref[...]
    i1 = route[:, 0:1]
    i2 = route[:, 1:2]
    w1 = route[:, 2:3]
    w2 = route[:, 3:4]
    first_is_low = i1 < i2
    y1 = y1_buf[slot]
    y2 = y2_buf[slot]
    ya = jnp.where(first_is_low, w1 * y1, w2 * y2)
    yb = jnp.where(first_is_low, w2 * y2, w1 * y1)
    y = _rmsnorm_rows(x_ref[...] + (ya + yb), g_ref[...])
    is_prompt = c < np_tiles

    @pl.when(is_prompt)
    def _():
        yp_ref[...] = y

    @pl.when(jnp.logical_not(is_prompt))
    def _():
        ys_ref[...] = y


def moe_combine_norm(x, route, yg, d1, d2, g, n_p, tc):
    n, d = x.shape
    nc = n // tc
    np_tiles = n_p // tc
    idx = pl.BlockSpec((1, 1, tc), lambda c: (c, 0, 0), memory_space=pltpu.SMEM)
    idx_next = pl.BlockSpec((1, 1, tc), lambda c: (jnp.minimum(c + 1, nc - 1), 0, 0),
                            memory_space=pltpu.SMEM)
    d1 = d1.reshape(nc, 1, tc)
    d2 = d2.reshape(nc, 1, tc)
    return pl.pallas_call(
        functools.partial(_combine_kernel, tc=tc, np_tiles=np_tiles),
        grid=(nc,),
        in_specs=[idx, idx, idx_next, idx_next, pl.BlockSpec((tc, d), lambda c: (c, 0)),
                  pl.BlockSpec((tc, ROUTER_LANES), lambda c: (c, 0)),
                  pl.BlockSpec((1, d), lambda c: (0, 0)), ANY_SPEC],
        out_specs=[pl.BlockSpec((tc, d), lambda c: (jnp.minimum(c, np_tiles - 1), 0)),
                   pl.BlockSpec((tc, d), lambda c: (jnp.maximum(c - np_tiles, 0), 0))],
        out_shape=[jax.ShapeDtypeStruct((n_p, d), F32), jax.ShapeDtypeStruct((n - n_p, d), F32)],
        scratch_shapes=[pltpu.VMEM((2, tc, d), F32), pltpu.VMEM((2, tc, d), F32),
                        pltpu.SemaphoreType.DMA((2,))],
        compiler_params=_cparams("arbitrary"),
        name="moe_combine_norm",
    )(d1, d2, d1, d2, x, route, g.reshape(1, d), yg)


def moe_plan(route):
    n = route.shape[0]
    sub = MOE_TILE // MOE_SUBS
    i1 = route[:, 0].astype(jnp.int32)
    i2 = route[:, 1].astype(jnp.int32)
    experts = jnp.arange(N_EXPERTS, dtype=jnp.int32)
    onehot = ((i1[:, None] == experts) | (i2[:, None] == experts)).astype(jnp.int32)
    counts = jnp.sum(onehot, axis=0)
    subs = (counts + sub - 1) // sub
    tiles = (counts + MOE_TILE - 1) // MOE_TILE
    tile_end = jnp.cumsum(tiles)
    tile_start = tile_end - tiles
    pos = jnp.cumsum(onehot, axis=0) - onehot
    dest = (tile_start * MOE_TILE)[None, :] + pos
    d1 = jnp.sum(jnp.where(i1[:, None] == experts, dest, 0), axis=1)
    d2 = jnp.sum(jnp.where(i2[:, None] == experts, dest, 0), axis=1)
    n_tiles = -(-(2 * n + N_EXPERTS * (MOE_TILE - 1)) // MOE_TILE)
    r = jnp.arange(n_tiles, dtype=jnp.int32)
    owner = jnp.minimum(jnp.sum((r[:, None] >= tile_end[None, :]).astype(jnp.int32), axis=1),
                        N_EXPERTS - 1)
    is_owner = owner[:, None] == experts[None, :]
    first_tile = jnp.sum(jnp.where(is_owner, tile_start[None, :], 0), axis=1)
    owner_subs = jnp.sum(jnp.where(is_owner, subs[None, :], 0), axis=1)
    n_used = tile_end[-1]
    n_sub = jnp.where(r < n_used, jnp.clip(owner_subs - (r - first_tile) * MOE_SUBS, 0, MOE_SUBS), 0)
    last_expert = jnp.sum(jnp.where(r == n_used - 1, owner, 0))
    tile_expert = jnp.where(r < n_used, owner, last_expert).astype(jnp.int32)
    tokens = jnp.arange(n, dtype=jnp.int32)
    row_token = jnp.zeros((n_tiles * MOE_TILE,), jnp.int32).at[jnp.concatenate([d1, d2])].set(
        jnp.concatenate([tokens, tokens]), unique_indices=True)
    return d1, d2, row_token, tile_expert, n_sub.astype(jnp.int32)


def _hgrn_kernel(p_ref, lb_ref, hn_ref, s0_ref, out_hbm, oa_ref, sout_ref,
                 st_ref, q_s, k_s, v_s, b_s, o_s, *, tt):
    del out_hbm
    c_len = HGRN_CHUNK
    t = pl.program_id(1)

    @pl.when(t == 0)
    def _():
        for h in range(N_HEADS):
            st_ref[h] = s0_ref[0, h].T

    lb = lb_ref[...]
    keep = 1.0 - lb
    z = p_ref[:, D_BRANCH:2 * D_BRANCH]
    e = jnp.exp(-jnp.abs(z))
    r = 1.0 / (1.0 + e)
    er = e * r
    logf = jnp.log(jnp.maximum(lb, LB_FLOOR) + keep * jnp.where(z >= 0.0, r, er))
    k_s[...] = keep * jnp.where(z >= 0.0, er, r)
    q_s[...] = _silu(p_ref[:, 0:D_BRANCH])
    v_s[...] = p_ref[:, 2 * D_BRANCH:3 * D_BRANCH]
    row = lax.broadcasted_iota(jnp.int32, (tt, tt), 0)
    col = lax.broadcasted_iota(jnp.int32, (tt, tt), 1)
    tri = jnp.where(((row ^ col) < c_len) & (col <= row), 1.0, 0.0).astype(BF16)
    lf_hi, lf_lo = _split_bf16(logf * LOG2_E)
    b_s[...] = _dot(tri, lf_hi) + _dot(tri, lf_lo)

    ones = jnp.ones((D_HEAD, D_HEAD), BF16)
    row_c = lax.broadcasted_iota(jnp.int32, (c_len, 1), 0)

    def chunk(c, carry):
        r0 = pl.multiple_of(c * c_len, c_len)
        qc = q_s[pl.ds(r0, c_len), :]
        kc = k_s[pl.ds(r0, c_len), :]
        vc = v_s[pl.ds(r0, c_len), :]
        bc = b_s[pl.ds(r0, c_len), :]
        b_last = bc[c_len - 1:c_len, :]
        qt = (qc * jnp.exp2(bc)).astype(BF16)
        kt = (kc * jnp.exp2(b_last - bc)).astype(BF16)
        decay = jnp.exp2(b_last)
        vb = vc.astype(BF16)
        xs = []
        for s in range(c_len):
            diff = jnp.where(row_c >= s, bc - bc[s:s + 1, :], NEG_BIG)
            xs.append((qc * kc[s:s + 1, :] * jnp.exp2(diff)).astype(BF16))
        x_all = jnp.concatenate(xs, axis=0)
        heads = [slice(h * D_HEAD, (h + 1) * D_HEAD) for h in range(N_HEADS)]
        states = [st_ref[h] for h in range(N_HEADS)]
        carried = [_dot_nt(qt[:, hs], st.astype(BF16)) for hs, st in zip(heads, states)]
        updates = [_dot_tn(vb[:, hs], kt[:, hs]) for hs in heads]
        sums = [_dot(x_all[:, hs], ones) for hs in heads]
        outs = []
        for hs, acc, y in zip(heads, carried, sums):
            for s in range(c_len):
                acc = acc + y[s * c_len:(s + 1) * c_len, :] * vc[s:s + 1, hs]
            outs.append(acc)
        for h, (hs, st, up) in enumerate(zip(heads, states, updates)):
            st_ref[h] = st * decay[:, hs] + up
        o_s[pl.ds(r0, c_len), :] = jnp.concatenate(outs, axis=1)
        return carry

    lax.fori_loop(0, tt // c_len, chunk, 0, unroll=HGRN_UNROLL)

    gate = _silu(p_ref[:, 3 * D_BRANCH:4 * D_BRANCH])
    hn = hn_ref[...]
    for h in range(N_HEADS):
        hs = slice(h * D_HEAD, (h + 1) * D_HEAD)
        oh = o_s[:, hs]
        oh = oh * lax.rsqrt(jnp.mean(oh * oh, axis=-1, keepdims=True) + EPS) * hn
        oa_ref[:, hs] = (oh * gate[:, hs]).astype(oa_ref.dtype)

    @pl.when(t == pl.num_programs(1) - 1)
    def _():
        for h in range(N_HEADS):
            sout_ref[0, h] = st_ref[h].T


def branch_buffer(n_rows):
    return jnp.zeros((n_rows, D_BRANCH), BF16)


def hgrn_mixer(proj, lb, hnorm, s0, row0, n_seq, seq_len, tt, out):
    nt = seq_len // tt
    rb0 = row0 // tt
    scr = [pltpu.VMEM((N_HEADS, D_HEAD, D_HEAD), F32)] + [pltpu.VMEM((tt, D_BRANCH), F32)] * 5
    return pl.pallas_call(
        functools.partial(_hgrn_kernel, tt=tt),
        grid=(n_seq, nt),
        in_specs=[pl.BlockSpec((tt, 4 * D_BRANCH), lambda s, t: (rb0 + s * nt + t, 0)),
                  pl.BlockSpec((1, D_BRANCH), lambda s, t: (0, 0)),
                  pl.BlockSpec((1, D_HEAD), lambda s, t: (0, 0)),
                  pl.BlockSpec((1, N_HEADS, D_HEAD, D_HEAD), lambda s, t: (s, 0, 0, 0)), ANY_SPEC],
        out_specs=[pl.BlockSpec((tt, D_BRANCH), lambda s, t: (rb0 + s * nt + t, 0)),
                   pl.BlockSpec((1, N_HEADS, D_HEAD, D_HEAD), lambda s, t: (s, 0, 0, 0))],
        out_shape=[jax.ShapeDtypeStruct(out.shape, out.dtype),
                   jax.ShapeDtypeStruct((n_seq, N_HEADS, D_HEAD, D_HEAD), F32)],
        scratch_shapes=scr,
        input_output_aliases={4: 0},
        compiler_params=_cparams("arbitrary", "arbitrary"),
        name="hgrn_mixer",
    )(proj, lb.reshape(1, D_BRANCH), hnorm.reshape(1, D_HEAD), s0, out)


def _pool_conv_kernel(p_ref, hp_ref, hc_ref, wp_ref, ps_ref, cw_ref, ob_hbm, oc_hbm,
                      ob_ref, oc_ref, pn_ref, cn_ref, xe, ue, *, tt, pos0):
    del ob_hbm, oc_hbm
    t = pl.program_id(1)

    @pl.when(t == 0)
    def _():
        xe[0:POOL_HALO, :] = hp_ref[0]
        ue[0:CONV_HALO, :] = hc_ref[0]

    @pl.when(t > 0)
    def _():
        xe[0:POOL_HALO, :] = xe[tt:tt + POOL_HALO, :]
        ue[0:CONV_HALO, :] = ue[tt:tt + CONV_HALO, :]

    x = p_ref[:, 0:D_BRANCH]
    xe[POOL_HALO:POOL_HALO + tt, :] = x
    u = p_ref[:, 3 * D_BRANCH:4 * D_BRANCH] * p_ref[:, D_BRANCH:2 * D_BRANCH]
    ue[CONV_HALO:CONV_HALO + tt, :] = u

    pos = pos0 + t * tt + lax.broadcasted_iota(jnp.int32, (tt, 1), 0)
    group = D_BRANCH // len(POOL_WINDOWS)
    for g, w in enumerate(POOL_WINDOWS):
        ls = slice(g * group, (g + 1) * group)
        s = x[:, ls]
        for j in range(1, w):
            s = s + xe[POOL_HALO - j:POOL_HALO - j + tt, ls]
        cnt = jnp.minimum(pos + 1, w).astype(F32)
        dlt = s / cnt - x[:, ls]
        y = _dot(dlt.astype(BF16), wp_ref[g].astype(BF16))
        ob_ref[:, ls] = (y * ps_ref[:, ls]).astype(ob_ref.dtype)

    y = ue[CONV_HALO - 2:CONV_HALO - 2 + tt, :] * cw_ref[0:1, :]
    for j in range(1, CONV_W):
        y = y + ue[CONV_HALO - 2 + j:CONV_HALO - 2 + j + tt, :] * cw_ref[j:j + 1, :]
    oc_ref[...] = (p_ref[:, 2 * D_BRANCH:3 * D_BRANCH] * y).astype(oc_ref.dtype)

    pn_ref[0] = xe[tt:tt + POOL_HALO, :]
    cn_ref[0] = ue[tt:tt + CONV_HALO, :]


def pool_conv_mixer(proj, hist_pool, hist_conv, w_pool, pool_scale, conv_w,
                    row0, n_seq, seq_len, tt, pos0, outs):
    nt = seq_len // tt
    rb0 = row0 // tt
    out_shape = jax.ShapeDtypeStruct(outs[0].shape, outs[0].dtype)
    cw = jnp.zeros((8, D_BRANCH), F32).at[:CONV_W].set(conv_w)
    out_row = pl.BlockSpec((tt, D_BRANCH), lambda s, t: (rb0 + s * nt + t, 0))
    return pl.pallas_call(
        functools.partial(_pool_conv_kernel, tt=tt, pos0=pos0),
        grid=(n_seq, nt),
        in_specs=[pl.BlockSpec((tt, 4 * D_BRANCH), lambda s, t: (rb0 + s * nt + t, 1)),
                  pl.BlockSpec((1, POOL_HALO, D_BRANCH), lambda s, t: (s, 0, 0)),
                  pl.BlockSpec((1, CONV_HALO, D_BRANCH), lambda s, t: (s, 0, 0)),
                  pl.BlockSpec(w_pool.shape, lambda s, t: (0, 0, 0)),
                  pl.BlockSpec((1, D_BRANCH), lambda s, t: (0, 0)),
                  pl.BlockSpec((8, D_BRANCH), lambda s, t: (0, 0)), ANY_SPEC, ANY_SPEC],
        out_specs=[out_row, out_row,
                   pl.BlockSpec((1, POOL_HALO, D_BRANCH), lambda s, t: (s, 0, 0)),
                   pl.BlockSpec((1, CONV_HALO, D_BRANCH), lambda s, t: (s, 0, 0))],
        out_shape=[out_shape, out_shape,
                   jax.ShapeDtypeStruct((n_seq, POOL_HALO, D_BRANCH), F32),
                   jax.ShapeDtypeStruct((n_seq, CONV_HALO, D_BRANCH), F32)],
        scratch_shapes=[pltpu.VMEM((tt + POOL_HALO, D_BRANCH), F32),
                        pltpu.VMEM((tt + CONV_HALO, D_BRANCH), F32)],
        input_output_aliases={6: 0, 7: 1},
        compiler_params=_cparams("arbitrary", "arbitrary"),
        name="pool_conv_mixer",
    )(proj, hist_pool, hist_conv, w_pool, pool_scale.reshape(1, D_BRANCH), cw, *outs)


def _sb_tile(q_ref, kv, o_acc, run, masked):
    tq = q_ref.shape[0]
    scale = D_HEAD ** -0.5
    kr = lax.broadcasted_iota(jnp.int32, (SB_TILE, 2 * SB_TILE), 0)
    kc = lax.broadcasted_iota(jnp.int32, (SB_TILE, 2 * SB_TILE), 1)
    sums = jnp.where((kc >= SB_TILE) | (kr > kc), 1.0, 0.0).astype(BF16)
    if masked:
        qi = lax.broadcasted_iota(jnp.int32, (tq, SB_TILE), 0)
        ki = lax.broadcasted_iota(jnp.int32, (tq, SB_TILE), 1)
        mask = ki < qi
    run_old = run[...]
    heads = [slice(h * D_HEAD, (h + 1) * D_HEAD) for h in range(N_HEADS)]
    kvs = [kv(h) for h in range(N_HEADS)]
    zs = [_dot_nt(q_ref[:, hs], k.astype(BF16)) * scale for hs, (k, _) in zip(heads, kvs)]
    lks = []
    for z in zs:
        log_keep = -(jnp.maximum(z, 0.0) + jnp.log1p(jnp.exp(-jnp.abs(z))))
        lks.append(jnp.where(mask, log_keep, 0.0) if masked else log_keep)
    splits = [_split_bf16(lk) for lk in lks]
    sms = [_dot(hi, sums) + _dot(lo, sums) for hi, lo in splits]
    probs = []
    for hs, z, lk, sm in zip(heads, zs, lks, sms):
        a = jnp.exp(z + lk + sm[:, 0:SB_TILE] + run_old[:, hs])
        probs.append((jnp.where(mask, a, 0.0) if masked else a).astype(BF16))
    o_new = [_dot(a, v.astype(BF16)) for a, (_, v) in zip(probs, kvs)]
    o_acc[...] += jnp.concatenate(o_new, axis=1)
    run[...] = run_old + jnp.concatenate([sm[:, SB_TILE:2 * SB_TILE] for sm in sms], axis=1)


def _sb_alive(run):
    return (jnp.max(run[...]) > SB_DONE).astype(jnp.int32)


def _sb_past_loop(q_ref, kv_tile, o_acc, run, n_tiles):
    def cond(c):
        j, alive = c
        return (j >= 0) & (alive > 0)

    def body(c):
        j, _ = c
        r0 = pl.multiple_of(j * SB_TILE, SB_TILE)
        _sb_tile(q_ref, functools.partial(kv_tile, r0), o_acc, run, masked=False)
        return j - 1, _sb_alive(run)

    lax.while_loop(cond, body, (jnp.int32(n_tiles) - 1, _sb_alive(run)))


def _sb_prompt_kernel(q_ref, k_ref, v_ref, out_hbm, o_ref, o_acc, run):
    del out_hbm
    i = pl.program_id(1)
    o_acc[...] = jnp.zeros_like(o_acc)
    run[...] = jnp.zeros_like(run)

    def kv_tile(r0, h):
        hs = slice(h * D_HEAD, (h + 1) * D_HEAD)
        return k_ref[pl.ds(r0, SB_TILE), hs], v_ref[pl.ds(r0, SB_TILE), hs]

    _sb_tile(q_ref, functools.partial(kv_tile, pl.multiple_of(i * SB_TILE, SB_TILE)), o_acc, run,
             masked=True)
    _sb_past_loop(q_ref, kv_tile, o_acc, run, i)
    o_ref[...] = o_acc[...].astype(o_ref.dtype)


def sb_prompt_mixer(qkv, n_seq, seq_len, out):
    nq = seq_len // SB_TILE
    seq = lambda c: pl.BlockSpec((seq_len, D_BRANCH), lambda s, i: (s, c))
    return pl.pallas_call(
        _sb_prompt_kernel,
        grid=(n_seq, nq),
        in_specs=[pl.BlockSpec((SB_TILE, D_BRANCH), lambda s, i: (s * nq + i, 0)), seq(1), seq(2),
                  ANY_SPEC],
        out_specs=pl.BlockSpec((SB_TILE, D_BRANCH), lambda s, i: (s * nq + i, 0)),
        out_shape=jax.ShapeDtypeStruct(out.shape, out.dtype),
        scratch_shapes=[pltpu.VMEM((SB_TILE, D_BRANCH), F32), pltpu.VMEM((SB_TILE, D_BRANCH), F32)],
        input_output_aliases={3: 0},
        compiler_params=_cparams("arbitrary", "arbitrary"),
        name="sb_prompt_mixer",
    )(qkv, qkv, qkv, out)


def _sb_sample_kernel(qkv_ref, kp_ref, vp_ref, out_hbm, o_ref, o_acc, run, *, n_past_tiles):
    del out_hbm
    o_acc[...] = jnp.zeros_like(o_acc)
    run[...] = jnp.zeros_like(run)
    tq = qkv_ref.shape[0]
    q_ref = qkv_ref.at[:, 0:D_BRANCH]
    pad = jnp.zeros((SB_TILE - tq, D_BRANCH), qkv_ref.dtype)
    k_new = jnp.concatenate([qkv_ref[:, D_BRANCH:2 * D_BRANCH], pad], axis=0)
    v_new = jnp.concatenate([qkv_ref[:, 2 * D_BRANCH:3 * D_BRANCH], pad], axis=0)

    def kv_new(h):
        hs = slice(h * D_HEAD, (h + 1) * D_HEAD)
        return k_new[:, hs], v_new[:, hs]

    def kv_past(r0, h):
        return kp_ref[pl.ds(r0, SB_TILE), h, :], vp_ref[pl.ds(r0, SB_TILE), h, :]

    _sb_tile(q_ref, kv_new, o_acc, run, masked=True)
    _sb_past_loop(q_ref, kv_past, o_acc, run, n_past_tiles)
    o_ref[...] = o_acc[...].astype(o_ref.dtype)


def sb_sample_mixer(qkv, k_cache, v_cache, layer, row0, n_seq, seq_len, out):
    past = k_cache.shape[2]
    rb0 = row0 // seq_len
    cache = pl.BlockSpec((None, None, past, N_HEADS, D_HEAD), lambda s: (layer, s, 0, 0, 0))
    return pl.pallas_call(
        functools.partial(_sb_sample_kernel, n_past_tiles=past // SB_TILE),
        grid=(n_seq,),
        in_specs=[pl.BlockSpec((seq_len, 3 * D_BRANCH), lambda s: (rb0 + s, 0)), cache, cache, ANY_SPEC],
        out_specs=pl.BlockSpec((seq_len, D_BRANCH), lambda s: (rb0 + s, 0)),
        out_shape=jax.ShapeDtypeStruct(out.shape, out.dtype),
        scratch_shapes=[pltpu.VMEM((seq_len, D_BRANCH), F32), pltpu.VMEM((seq_len, D_BRANCH), F32)],
        input_output_aliases={3: 0},
        compiler_params=_cparams("arbitrary"),
        name="sb_sample_mixer",
    )(qkv, k_cache, v_cache, out)


def _tile(n, want):
    if n <= want:
        return n
    for t in range(want, 7, -8):
        if n % t == 0:
            return t
    return n


def kernel(x_prompt, x_sample, state_hgrn, state_pool, state_conv, cache_k, cache_v, norm_mix, norm_ffn, norm_final, w_in, hgrn_lower_bound, hgrn_out_norm, pool_w, pool_scale, conv_w, w_branch, w_gate, b_gate, w_out, ffn_w1, ffn_w3, ffn_w2, moe_router, moe_w1, moe_w3, moe_w2):
    depth = w_in.shape[0]
    pb, pt, d = x_prompt.shape
    sb, st, _ = x_sample.shape
    past = cache_k.shape[2]
    n_p = pb * pt
    n_s = sb * st
    n = n_p + n_s
    assert depth % 2 == 0, "the last layer is expected to be a MoE layer"
    x = (x_prompt.reshape(n_p, d), x_sample.reshape(n_s, d))

    tm_row = _tile(n_s, 512)
    tm_mm = _tile(n_s, 2048)
    tm_half = tm_mm // 2
    tc = _tile(n_s, 256)
    tt_p = _tile(pt, 256)
    assert n_p % tm_mm == 0 and n_p % tc == 0 and n_p % tm_row == 0

    sm = jax.nn.softmax(hgrn_lower_bound.astype(F32), axis=0)
    lbs = jnp.cumsum(sm, axis=0) - sm[0:1]
    ffn_w2_bf16 = ffn_w2.astype(BF16)

    zeros_state = jnp.zeros((pb, N_HEADS, D_HEAD, D_HEAD), F32)
    zeros_pool = jnp.zeros((pb, POOL_HALO, D_BRANCH), F32)
    zeros_conv = jnp.zeros((pb, CONV_HALO, D_BRANCH), F32)
    pool_pad = ((0, 0), (0, 0), (POOL_HALO - state_pool.shape[2], 0), (0, 0))
    conv_pad = ((0, 0), (0, 0), (CONV_HALO - state_conv.shape[2], 0), (0, 0))
    sample_pool = jnp.pad(state_pool, pool_pad)
    sample_conv = jnp.pad(state_conv, conv_pad)

    outs = {k: [] for k in ("p_h", "p_pool", "p_conv", "s_h", "s_pool", "s_conv")}
    caches = cache_buffers(depth, n_p, n_s)
    xn = rmsnorm(x, norm_mix[0], BF16, tm_row)
    y_p = y_s = None
    for l in range(depth):
        proj = in_proj(xn, w_in, l, 8 * D_BRANCH, tm_mm, D_BRANCH)
        qkv16, *caches = qkv_proj(xn, w_in, l, 8 * D_BRANCH, n_p, caches, tm_half)

        oa, hs_p = hgrn_mixer(proj, lbs[l], hgrn_out_norm[l], zeros_state, 0, pb, pt, tt_p,
                              branch_buffer(n))
        oa, hs_s = hgrn_mixer(proj, lbs[l], hgrn_out_norm[l], state_hgrn[l], n_p, sb, st, st, oa)
        ob, oc, pool_p, conv_p = pool_conv_mixer(
            proj, zeros_pool, zeros_conv, pool_w[l], pool_scale[l], conv_w[l], 0, pb, pt, tt_p, 0,
            (branch_buffer(n), branch_buffer(n)))
        ob, oc, pool_s, conv_s = pool_conv_mixer(
            proj, sample_pool[l], sample_conv[l], pool_w[l], pool_scale[l], conv_w[l],
            n_p, sb, st, st, past, (ob, oc))
        od = sb_prompt_mixer(qkv16, pb, pt, branch_buffer(n))
        od = sb_sample_mixer(qkv16, cache_k, cache_v, l, n_p, sb, st, od)

        merged = gate_merge(xn, (oa, ob, oc, od), w_gate, b_gate, w_branch, l, tm_half, 256)
        if l % 2 == 0:
            x, h = mix_out(merged, w_out[l].astype(BF16), x, norm_ffn[l], tm_row)
        else:
            x = out_proj(merged, w_out, l, x, tm_mm, D_BRANCH)

        outs["p_h"].append(hs_p)
        outs["s_h"].append(hs_s)
        outs["p_pool"].append(pool_p[:, 1:])
        outs["s_pool"].append(pool_s[:, 1:])
        outs["p_conv"].append(conv_p[:, CONV_HALO - (CONV_W - 1):])
        outs["s_conv"].append(conv_s[:, CONV_HALO - (CONV_W - 1):])

        if l % 2 == 0:
            act = dense_glu(h, ffn_w1, ffn_w3, l // 2, tm_mm, D_BRANCH)
            x = out_proj(act, ffn_w2_bf16, l // 2, x, tm_half, D_BRANCH)
            xn = rmsnorm(x, norm_mix[l + 1], BF16, tm_row)
        else:
            h32, route = rmsnorm_router(x, norm_ffn[l], moe_router[l // 2], tm_row)
            d1, d2, row_token, tile_expert, n_sub = moe_plan(route)
            act = grouped_glu(h32, row_token, tile_expert, n_sub, moe_w1, moe_w3, l // 2,
                              D_BRANCH)
            yg = grouped_down(act, tile_expert, n_sub, moe_w2, l // 2, MOE_TILE // 2, 256)
            if l + 1 < depth:
                raise NotImplementedError("a MoE layer followed by another layer")
            y_p, y_s = moe_combine_norm(x, route, yg, d1, d2, norm_final, n_p, tc)

    pk, pv, sk, sv = caches
    st_ = lambda k: jnp.stack(outs[k])
    return (y_p.reshape(pb, pt, d), y_s.reshape(sb, st, d),
            st_("p_h"), st_("p_pool"), st_("p_conv"),
            pk.reshape(depth, pb, pt, N_HEADS, D_HEAD), pv.reshape(depth, pb, pt, N_HEADS, D_HEAD),
            st_("s_h"), st_("s_pool"), st_("s_conv"),
            sk.reshape(depth, sb, st, N_HEADS, D_HEAD), sv.reshape(depth, sb, st, N_HEADS, D_HEAD))
```
